```python
import jax
import jax.numpy as jnp
from jax import lax
import numpy as np

D_MODEL = 1024
BATCH = 2
SEQ = 8192
DEPTH = 2

GRID_W = 64
CTX_LEN = 256
N_MOD = 9
D_FF = ((8 * D_MODEL // 3 + 127) // 128) * 128
EPS = 1e-6

N_GROUPS = 4
GROUP_W = D_MODEL // N_GROUPS
MIX_W = N_GROUPS * GROUP_W

CONV_K = 31
LN_EPS = 1e-5
LRU_BLOCKS = 4
LRU_CONV_K = 4
LRU_C = 8.0
RWKV_HEAD = 64
RWKV_HEADS = GROUP_W // RWKV_HEAD
DECAY_LORA = 64
AAA_LORA = 64
GATE_LORA = 128
GN_EPS = 64e-5
QK_NOPE = 64
QK_ROPE = 32
V_HEAD = 64
MLA_HEADS = GROUP_W // V_HEAD
Q_LORA = 256
KV_LORA = 128
ROPE_BASE = 10000.0
Q_BLOCK = 128
SM_SCALE = (QK_NOPE + QK_ROPE) ** -0.5

A_COLS = 2 * GROUP_W
B_COLS = 2 * GROUP_W
C_COLS = 3 * GROUP_W + DECAY_LORA + AAA_LORA + GATE_LORA
D_COLS = Q_LORA + KV_LORA + QK_ROPE
IN_COLS = A_COLS + B_COLS + C_COLS + D_COLS
IN_SPLITS = (A_COLS, A_COLS + B_COLS, A_COLS + B_COLS + C_COLS)
RWKV_SPLITS = (GROUP_W, 2 * GROUP_W, 3 * GROUP_W, 3 * GROUP_W + DECAY_LORA,
               3 * GROUP_W + DECAY_LORA + AAA_LORA)

kernel_name = 'hybrid_parallel_group_dit_block'


def rms_norm(x, gain=None):
    xf = x.astype(jnp.float32)
    y = xf * lax.rsqrt(jnp.mean(xf * xf, axis=-1, keepdims=True) + EPS)
    if gain is not None:
        y = y * gain.astype(jnp.float32)
    return y.astype(x.dtype)


def layer_norm(x, gain, bias):
    xf = x.astype(jnp.float32)
    mu = jnp.mean(xf, axis=-1, keepdims=True)
    var = jnp.mean(jnp.square(xf - mu), axis=-1, keepdims=True)
    y = (xf - mu) * lax.rsqrt(var + LN_EPS)
    return (y * gain.astype(jnp.float32) + bias.astype(jnp.float32)).astype(x.dtype)


def group_norm_heads(y, gain, bias):
    yf = y.astype(jnp.float32)
    mu = jnp.mean(yf, axis=-1, keepdims=True)
    var = jnp.mean(jnp.square(yf - mu), axis=-1, keepdims=True)
    yn = (yf - mu) * lax.rsqrt(var + GN_EPS)
    g = gain.astype(jnp.float32).reshape(RWKV_HEADS, RWKV_HEAD)
    b = bias.astype(jnp.float32).reshape(RWKV_HEADS, RWKV_HEAD)
    return (yn * g + b).astype(y.dtype)


def l2_normalize(t):
    tf = t.astype(jnp.float32)
    return (tf * lax.rsqrt(jnp.maximum(jnp.sum(tf * tf, axis=-1, keepdims=True), 1e-24))).astype(t.dtype)


def modulate(x, shift, scale):
    return rms_norm(x) * (1 + scale) + shift


def swiglu(h, w13, w2):
    gate, up = jnp.split(h @ w13, 2, axis=-1)
    return (jax.nn.silu(gate) * up) @ w2


def depthwise_conv(x, w, b, pad_left, pad_right):
    y = lax.conv_general_dilated(
        x, w[:, None, :].astype(x.dtype), window_strides=(1,),
        padding=((pad_left, pad_right),), dimension_numbers=('NWC', 'WIO', 'NWC'),
        feature_group_count=x.shape[-1])
    return y + b


def token_shift(u, mu_prev, mu_next):
    zero = jnp.zeros_like(u[:, :1])
    prev = jnp.concatenate([zero, u[:, :-1]], axis=1)
    nxt = jnp.concatenate([u[:, 1:], zero], axis=1)
    return u + mu_prev * (prev - u) + mu_next * (nxt - u)


def linear_scan(a, b, h0, reverse):
    def combine(e1, e2):
        a1, b1 = e1
        a2, b2 = e2
        return a1 * a2, a2 * b1 + b2
    a_cum, h = lax.associative_scan(combine, (a, b), reverse=reverse, axis=1)
    return h + a_cum * h0[:, None, :]


def wkv7_scan(r, decay, k, v, kk, b, s0, reverse):
    def step(s, inp):
        r_t, w_t, k_t, v_t, kk_t, b_t = inp
        sa = jnp.einsum('bhvk,bhk->bhv', s, kk_t)
        s = s * w_t[:, :, None, :] - sa[..., None] * b_t[:, :, None, :] + v_t[..., None] * k_t[:, :, None, :]
        return s, jnp.einsum('bhvk,bhk->bhv', s, r_t)
    xs = tuple(jnp.moveaxis(t, 1, 0) for t in (r, decay, k, v, kk, b))
    s_final, ys = lax.scan(step, s0, xs, reverse=reverse)
    return jnp.moveaxis(ys, 0, 1), s_final


def axial_rope_tables(t_len, dtype):
    t = jnp.arange(t_len, dtype=jnp.int32)
    rows = (t // GRID_W).astype(jnp.float32)
    cols = (t % GRID_W).astype(jnp.float32)
    n_freq = QK_ROPE // 4
    inv_freq = ROPE_BASE ** (-jnp.arange(n_freq, dtype=jnp.float32) / n_freq)
    ang = jnp.stack([rows[:, None] * inv_freq, cols[:, None] * inv_freq], axis=1)
    ang = jnp.concatenate([ang, ang], axis=-1).reshape(t_len, QK_ROPE)
    return jnp.cos(ang).astype(dtype), jnp.sin(ang).astype(dtype)


def apply_rope(x, cos, sin):
    xs = x.reshape(x.shape[:-1] + (2, 2, QK_ROPE // 4))
    rot = jnp.stack([-xs[..., 1, :], xs[..., 0, :]], axis=-2).reshape(x.shape)
    return x * cos + rot * sin


def conformer_conv(u, dw_w, dw_b, ln_g, ln_b):
    val, gate = jnp.split(u, 2, axis=-1)
    z = val * jax.nn.sigmoid(gate)
    z = depthwise_conv(z, dw_w, dw_b, CONV_K // 2, CONV_K // 2)
    return jax.nn.silu(layer_norm(z, ln_g, ln_b))


def rglru_mixer(u, uc, need_ctx_out, conv_w, conv_b, w_a, b_a, w_x, b_x, lam):
    pad_l = LRU_CONV_K // 2
    pad_r = LRU_CONV_K - 1 - pad_l

    def prepare(v):
        xb, gb = jnp.split(v, 2, axis=-1)
        return depthwise_conv(xb, conv_w, conv_b, pad_l, pad_r), gb

    def gates(xv, d):
        xh = xv.reshape(xv.shape[:-1] + (LRU_BLOCKS, GROUP_W // LRU_BLOCKS))
        r = jax.nn.sigmoid(jnp.einsum('bthi,hij->bthj', xh, w_a[d]).reshape(xv.shape) + b_a[d])
        i = jax.nn.sigmoid(jnp.einsum('bthi,hij->bthj', xh, w_x[d]).reshape(xv.shape) + b_x[d])
        log_a = -LRU_C * r * jax.nn.softplus(-lam[d])
        return jnp.exp(log_a), jnp.sqrt(-jnp.expm1(2 * log_a)) * (i * xv)

    xl, gl = prepare(u)
    xc, gc = prepare(uc)
    outs_l, outs_c = [], []
    for d, rev in ((0, False), (1, True)):
        a_c, b_c = gates(xc, d)
        h_c = linear_scan(a_c, b_c, jnp.zeros_like(xc[:, 0]), rev)
        h_final = h_c[:, 0] if rev else h_c[:, -1]
        a_l, b_l = gates(xl, d)
        outs_l.append(linear_scan(a_l, b_l, h_final, rev))
        outs_c.append(h_c)
    y_l = (outs_l[0] + outs_l[1]) * jax.nn.gelu(gl)
    y_c = (outs_c[0] + outs_c[1]) * jax.nn.gelu(gc) if need_ctx_out else None
    return y_l, y_c


def rwkv7_mixer(u, uc, need_ctx_out, mu_prev, mu_next, w0, w_up, a0, a_up, g_up,
                k_k, k_a, r_k, gn_g, gn_b):
    def heads(t):
        return t.reshape(t.shape[:-1] + (RWKV_HEADS, RWKV_HEAD))

    def prepare(v):
        v = token_shift(v, mu_prev, mu_next)
        r, k, val, wd, ad, gd = jnp.split(v, RWKV_SPLITS, axis=-1)
        return heads(r), k, heads(val), jnp.tanh(wd), ad, gd, l2_normalize(heads(k * k_k))

    def run(p, d, reverse, s0):
        r, k, v, wd, ad, _, kk = p
        decay = jnp.exp(-jnp.exp(-jax.nn.softplus(-(w0[d] + wd @ w_up[d])) - 0.5))
        a = jax.nn.sigmoid(a0[d] + ad @ a_up[d])
        k_d = heads(k * (1 + (a - 1) * k_a))
        y, s = wkv7_scan(r, heads(decay), k_d, v, kk, kk * heads(a), s0, reverse)
        bonus = jnp.sum(r * k_d * r_k, axis=-1, keepdims=True) * v
        return y, bonus, s

    def readout(p, ys, bonuses):
        g = jax.nn.sigmoid(p[5]) @ g_up
        o = group_norm_heads(ys[0] + ys[1], gn_g, gn_b) + bonuses[0] + bonuses[1]
        return o.reshape(o.shape[:-2] + (GROUP_W,)) * g

    p_l = prepare(u)
    p_c = prepare(uc)
    s0 = jnp.zeros((u.shape[0], RWKV_HEADS, RWKV_HEAD, RWKV_HEAD), u.dtype)
    ys_l, bs_l, ys_c, bs_c = [], [], [], []
    for d, rev in ((0, False), (1, True)):
        y_c, b_c, s_c = run(p_c, d, rev, s0)
        y_l, b_l, _ = run(p_l, d, rev, s_c)
        ys_l.append(y_l)
        bs_l.append(b_l)
        ys_c.append(y_c)
        bs_c.append(b_c)
    out_l = readout(p_l, ys_l, bs_l)
    out_c = readout(p_c, ys_c, bs_c) if need_ctx_out else None
    return out_l, out_c


def mla_mixer(u, uc, need_ctx_out, q_norm, w_uq, kv_norm, w_ukv, cos, sin):
    def queries(v):
        cq = v[..., :Q_LORA]
        return (rms_norm(cq, q_norm) @ w_uq).reshape(v.shape[:2] + (MLA_HEADS, QK_NOPE + QK_ROPE))

    def keys_values(v):
        ckv = v[..., Q_LORA:Q_LORA + KV_LORA]
        k_rope = v[..., Q_LORA + KV_LORA:]
        kv = (rms_norm(ckv, kv_norm) @ w_ukv).reshape(v.shape[:2] + (MLA_HEADS, QK_NOPE + V_HEAD))
        return kv[..., :QK_NOPE], kv[..., QK_NOPE:], k_rope

    def assemble_k(k_nope, k_rope):
        k_rope = jnp.broadcast_to(k_rope[:, :, None, :], k_nope.shape[:-1] + (QK_ROPE,))
        return jnp.concatenate([k_nope, k_rope], axis=-1)

    def attend(q, k, v):
        s = jnp.einsum('bqhd,bkhd->bhqk', q, k, preferred_element_type=jnp.float32) * SM_SCALE
        p = jax.nn.softmax(s, axis=-1).astype(v.dtype)
        return jnp.einsum('bhqk,bkhd->bqhd', p, v)

    q = queries(u)
    q = jnp.concatenate([q[..., :QK_NOPE],
                         apply_rope(q[..., QK_NOPE:], cos[:, None, :], sin[:, None, :])], axis=-1)
    k_nope, v, k_rope = keys_values(u)
    k = assemble_k(k_nope, apply_rope(k_rope, cos, sin))
    k_nope_c, v_c, k_rope_c = keys_values(uc)
    k_c = assemble_k(k_nope_c, k_rope_c)
    k_all = jnp.concatenate([k, k_c], axis=1)
    v_all = jnp.concatenate([v, v_c], axis=1)
    bsz, t_len = q.shape[0], q.shape[1]
    q_blocks = jnp.moveaxis(q.reshape(bsz, t_len // Q_BLOCK, Q_BLOCK, MLA_HEADS, QK_NOPE + QK_ROPE), 1, 0)
    o = lax.map(lambda qb: attend(qb, k_all, v_all), q_blocks)
    o = jnp.moveaxis(o, 0, 1).reshape(bsz, t_len, MLA_HEADS * V_HEAD)
    o_c = None
    if need_ctx_out:
        o_c = attend(queries(uc), k_c, v_c).reshape(bsz, uc.shape[1], MLA_HEADS * V_HEAD)
    return o, o_c


def setup_inputs(seed: int = 0) -> dict:
    key = jax.random.key(seed)
    keys = list(jax.random.split(key, 48))

    def nrm(shape, scale):
        return jax.random.normal(keys.pop(), shape, jnp.float32) * scale

    def unif(shape, lo, hi):
        return jax.random.uniform(keys.pop(), shape, jnp.float32, lo, hi)

    L, D, G = DEPTH, D_MODEL, GROUP_W
    a8 = unif((L, 2, G), 0.9, 0.999)
    s = a8 ** (1.0 / LRU_C)
    lru_lambda = jnp.log(s) - jnp.log1p(-s)
    return {
        'x': nrm((BATCH, SEQ, D), 1.0),
        'c': nrm((BATCH, D), 1.0),
        'ctx': nrm((BATCH, CTX_LEN, D), 1.0),
        'c_ctx': nrm((D,), 1.0),
        'ada_w': nrm((L, D, N_MOD * D), 0.5 * D ** -0.5),
        'ada_b': nrm((L, N_MOD * D), 0.02),
        'ffn1_w13': nrm((L, D, 2 * D_FF), D ** -0.5),
        'ffn1_w2': nrm((L, D_FF, D), D_FF ** -0.5),
        'ffn2_w13': nrm((L, D, 2 * D_FF), D ** -0.5),
        'ffn2_w2': nrm((L, D_FF, D), D_FF ** -0.5),
        'w_in': nrm((L, D, IN_COLS), D ** -0.5),
        'w_out': nrm((L, MIX_W, D), MIX_W ** -0.5),
        'cv_dw_w': nrm((L, CONV_K, G), CONV_K ** -0.5),
        'cv_dw_b': nrm((L, G), 0.02),
        'cv_ln_g': 1.0 + nrm((L, G), 0.02),
        'cv_ln_b': nrm((L, G), 0.02),
        'lru_conv_w': nrm((L, LRU_CONV_K, G), LRU_CONV_K ** -0.5),
        'lru_conv_b': nrm((L, G), 0.02),
        'lru_wa': nrm((L, 2, LRU_BLOCKS, G // LRU_BLOCKS, G // LRU_BLOCKS), (G // LRU_BLOCKS) ** -0.5),
        'lru_ba': nrm((L, 2, G), 0.1),
        'lru_wx': nrm((L, 2, LRU_BLOCKS, G // LRU_BLOCKS, G // LRU_BLOCKS), (G // LRU_BLOCKS) ** -0.5),
        'lru_bx': nrm((L, 2, G), 0.1),
        'lru_lambda': lru_lambda,
        'rwkv_mu_prev': unif((L, C_COLS), 0.0, 0.5),
        'rwkv_mu_next': unif((L, C_COLS), 0.0, 0.5),
        'rwkv_w0': unif((L, 2, G), -6.0, 0.0),
        'rwkv_w_up': nrm((L, 2, DECAY_LORA, G), 0.1),
        'rwkv_a0': nrm((L, 2, G), 0.1),
        'rwkv_a_up': nrm((L, 2, AAA_LORA, G), 0.5 * AAA_LORA ** -0.5),
        'rwkv_g_up': nrm((L, GATE_LORA, G), GATE_LORA ** -0.5),
        'rwkv_k_k': 0.85 + nrm((L, G), 0.02),
        'rwkv_k_a': 1.0 + nrm((L, G), 0.02),
        'rwkv_r_k': nrm((L, RWKV_HEADS, RWKV_HEAD), 0.1),
        'rwkv_gn_g': 1.0 + nrm((L, G), 0.02),
        'rwkv_gn_b': nrm((L, G), 0.02),
        'mla_q_norm': 1.0 + nrm((L, Q_LORA), 0.02),
        'mla_w_uq': nrm((L, Q_LORA, MLA_HEADS * (QK_NOPE + QK_ROPE)), Q_LORA ** -0.5),
        'mla_kv_norm': 1.0 + nrm((L, KV_LORA), 0.02),
        'mla_w_ukv': nrm((L, KV_LORA, MLA_HEADS * (QK_NOPE + V_HEAD)), KV_LORA ** -0.5),
        'final_norm': 1.0 + nrm((D,), 0.02),
    }


def reference(x, c, ctx, c_ctx, ada_w, ada_b, ffn1_w13, ffn1_w2, ffn2_w13, ffn2_w2, w_in, w_out,
              cv_dw_w, cv_dw_b, cv_ln_g, cv_ln_b,
              lru_conv_w, lru_conv_b, lru_wa, lru_ba, lru_wx, lru_bx, lru_lambda,
              rwkv_mu_prev, rwkv_mu_next, rwkv_w0, rwkv_w_up, rwkv_a0, rwkv_a_up, rwkv_g_up,
              rwkv_k_k, rwkv_k_a, rwkv_r_k, rwkv_gn_g, rwkv_gn_b,
              mla_q_norm, mla_w_uq, mla_kv_norm, mla_w_ukv, final_norm):
    bsz, t_len, d_model = x.shape
    cos, sin = axial_rope_tables(t_len, x.dtype)
    silu_c = jax.nn.silu(c)
    silu_cc = jax.nn.silu(c_ctx)
    xc = ctx
    for l in range(DEPTH):
        need_ctx_out = l < DEPTH - 1
        mod = (silu_c @ ada_w[l] + ada_b[l]).reshape(bsz, N_MOD, 1, d_model)
        sh1, s1, g1, sh2, s2, g2, sh3, s3, g3 = [mod[:, i] for i in range(N_MOD)]
        mod_c = (silu_cc @ ada_w[l] + ada_b[l]).reshape(N_MOD, d_model)
        sh1c, s1c, g1c, sh2c, s2c, g2c, sh3c, s3c, g3c = [mod_c[i] for i in range(N_MOD)]

        x = x + 0.5 * g1 * swiglu(modulate(x, sh1, s1), ffn1_w13[l], ffn1_w2[l])
        xc = xc + 0.5 * g1c * swiglu(modulate(xc, sh1c, s1c), ffn1_w13[l], ffn1_w2[l])

        u = modulate(x, sh2, s2) @ w_in[l]
        uc = modulate(xc, sh2c, s2c) @ w_in[l]
        u_a, u_b, u_c, u_d = jnp.split(u, IN_SPLITS, axis=-1)
        uc_a, uc_b, uc_c, uc_d = jnp.split(uc, IN_SPLITS, axis=-1)

        y_a = conformer_conv(u_a, cv_dw_w[l], cv_dw_b[l], cv_ln_g[l], cv_ln_b[l])
        y_b, yc_b = rglru_mixer(u_b, uc_b, need_ctx_out, lru_conv_w[l], lru_conv_b[l],
                                lru_wa[l], lru_ba[l], lru_wx[l], lru_bx[l], lru_lambda[l])
        y_c, yc_c = rwkv7_mixer(u_c, uc_c, need_ctx_out, rwkv_mu_prev[l], rwkv_mu_next[l],
                                rwkv_w0[l], rwkv_w_up[l], rwkv_a0[l], rwkv_a_up[l], rwkv_g_up[l],
                                rwkv_k_k[l], rwkv_k_a[l], rwkv_r_k[l], rwkv_gn_g[l], rwkv_gn_b[l])
        y_d, yc_d = mla_mixer(u_d, uc_d, need_ctx_out, mla_q_norm[l], mla_w_uq[l],
                              mla_kv_norm[l], mla_w_ukv[l], cos, sin)
        y = jnp.concatenate([y_a, y_b, y_c, y_d], axis=-1) @ w_out[l]
        x = x + g2 * y

        if need_ctx_out:
            yc_a = conformer_conv(uc_a, cv_dw_w[l], cv_dw_b[l], cv_ln_g[l], cv_ln_b[l])
            yc = jnp.concatenate([yc_a, yc_b, yc_c, yc_d], axis=-1) @ w_out[l]
            xc = xc + g2c * yc
            xc = xc + 0.5 * g3c * swiglu(modulate(xc, sh3c, s3c), ffn2_w13[l], ffn2_w2[l])

        x = x + 0.5 * g3 * swiglu(modulate(x, sh3, s3), ffn2_w13[l], ffn2_w2[l])
    return rms_norm(x, final_norm)
```

```python
import functools

import jax
import jax.numpy as jnp
from jax import lax
from jax.experimental import pallas as pl
from jax.experimental.pallas import tpu as pltpu

F32 = jnp.float32
BF16 = jnp.bfloat16

GRID_W = 64
N_MOD = 9
EPS = 1e-6
GROUP_W = 256
CONV_K = 31
LN_EPS = 1e-5
LRU_CONV_K = 4
LRU_C = 8.0
RWKV_HEAD = 64
RWKV_HEADS = 4
GN_EPS = 64e-5
QK_NOPE = 64
QK_ROPE = 32
V_HEAD = 64
MLA_HEADS = 4
Q_LORA = 256
KV_LORA = 128
ROPE_BASE = 10000.0
SM_SCALE = (QK_NOPE + QK_ROPE) ** -0.5

LANE = 128
SUBLANE = 8
ROW_TILE = 512
CHUNK = 256
SUB = 64
CONV_HALO = 16
VMEM_LIMIT = 48 * 1024 * 1024


def _cparams(sem):
    return pltpu.CompilerParams(dimension_semantics=sem, vmem_limit_bytes=VMEM_LIMIT)


def _sigmoid(x):
    return 1.0 / (1.0 + jnp.exp(-x))


def _softplus(x):
    return jnp.maximum(x, 0.0) + jnp.log1p(jnp.exp(-jnp.abs(x)))


def _gelu_tanh(x):
    return 0.5 * x * (1.0 + jnp.tanh(0.7978845608028654 * (x + 0.044715 * (x * x * x))))


def _mm(a, b):
    return jnp.dot(a.astype(BF16), b.astype(BF16), preferred_element_type=F32)


def _mm_nt(a, b):
    return lax.dot_general(a.astype(BF16), b.astype(BF16), (((1,), (1,)), ((), ())),
                           preferred_element_type=F32)


def _split2(a):
    hi = a.astype(BF16)
    lo = (a - hi.astype(F32)).astype(BF16)
    return hi, lo


def _split3(a):
    hi = a.astype(BF16)
    r1 = a - hi.astype(F32)
    mid = r1.astype(BF16)
    lo = (r1 - mid.astype(F32)).astype(BF16)
    return hi, mid, lo


def _mm_exact_rhs(a, b_exact):
    hi, mid, lo = _split3(a)
    d = functools.partial(jnp.dot, preferred_element_type=F32)
    return d(hi, b_exact) + d(mid, b_exact) + d(lo, b_exact)


def _mm_exact_lhs(a_exact, b):
    hi, mid, lo = _split3(b)
    d = functools.partial(jnp.dot, preferred_element_type=F32)
    return d(a_exact, hi) + d(a_exact, mid) + d(a_exact, lo)


def _mm3(a, b):
    ah, al = _split2(a)
    bh, bl = _split2(b)
    d = functools.partial(jnp.dot, preferred_element_type=F32)
    return d(ah, bh) + d(ah, bl) + d(al, bh)


def _mm3_nt(a, b):
    ah, al = _split2(a)
    bh, bl = _split2(b)
    d = functools.partial(lax.dot_general, dimension_numbers=(((1,), (1,)), ((), ())),
                          preferred_element_type=F32)
    return d(ah, bh) + d(ah, bl) + d(al, bh)


def _mm3_tn(a, b):
    ah, al = _split2(a)
    bh, bl = _split2(b)
    d = functools.partial(lax.dot_general, dimension_numbers=(((0,), (0,)), ((), ())),
                          preferred_element_type=F32)
    return d(ah, bh) + d(ah, bl) + d(al, bh)


def _modulate(x, shift, scale):
    ms = jnp.mean(x * x, axis=-1, keepdims=True)
    return x * lax.rsqrt(ms + EPS) * (1.0 + scale) + shift


def _ada_kernel(c_ref, w_ref, b_ref, o_ref):
    c = c_ref[...]
    s = c * _sigmoid(c)
    o_ref[...] = _mm(s, w_ref[...]) + b_ref[...]


def _ada_call(cvec, ada_w, ada_b):
    n_layers, d, nd = ada_w.shape
    tn = nd // 8
    return pl.pallas_call(
        _ada_kernel,
        out_shape=jax.ShapeDtypeStruct((n_layers, SUBLANE, nd), F32),
        grid=(n_layers, nd // tn),
        in_specs=[
            pl.BlockSpec((SUBLANE, d), lambda l, j: (0, 0)),
            pl.BlockSpec((None, d, tn), lambda l, j: (l, 0, j)),
            pl.BlockSpec((None, 1, tn), lambda l, j: (l, 0, j)),
        ],
        out_specs=pl.BlockSpec((None, SUBLANE, tn), lambda l, j: (l, 0, j)),
        compiler_params=_cparams(("parallel", "parallel")),
        name="ada_mod",
    )(cvec, ada_w, ada_b.reshape(n_layers, 1, nd))


def _ffn_kernel(x_ref, mod_ref, wg_ref, wu_ref, w2_ref, o_ref, xm_scr, acc_scr, *, row0, n_ff):
    j = pl.program_id(1)

    @pl.when(j == 0)
    def _():
        x = x_ref[...]
        xm = _modulate(x, mod_ref[row0:row0 + 1, :], mod_ref[row0 + 1:row0 + 2, :])
        xm_scr[...] = xm.astype(BF16)
        acc_scr[...] = jnp.zeros_like(acc_scr)

    xm = xm_scr[...]
    g = jnp.dot(xm, wg_ref[...], preferred_element_type=F32)
    u = jnp.dot(xm, wu_ref[...], preferred_element_type=F32)
    a = (g * _sigmoid(g)) * u
    acc_scr[...] += jnp.dot(a.astype(BF16), w2_ref[...], preferred_element_type=F32)

    @pl.when(j == n_ff - 1)
    def _():
        gate = mod_ref[row0 + 2:row0 + 3, :]
        o_ref[...] = x_ref[...] + 0.5 * gate * acc_scr[...]


def _group_of_tile(i, tiles_per_batch):
    return (i + tiles_per_batch - 1) // tiles_per_batch


def _ffn_call(x, mods, w13, w2, *, row0, tiles_per_batch):
    n, d = x.shape
    d_ff = w2.shape[0]
    n_ff = 2
    tf = d_ff // n_ff
    assert tf * n_ff == d_ff and tf % LANE == 0
    kern = functools.partial(_ffn_kernel, row0=row0, n_ff=n_ff)
    return pl.pallas_call(
        kern,
        out_shape=jax.ShapeDtypeStruct((n, d), F32),
        grid=(n // ROW_TILE, n_ff),
        in_specs=[
            pl.BlockSpec((ROW_TILE, d), lambda i, j: (i, 0)),
            pl.BlockSpec((None, N_MOD, d), lambda i, j: (_group_of_tile(i, tiles_per_batch), 0, 0)),
            pl.BlockSpec((d, tf), lambda i, j: (0, j)),
            pl.BlockSpec((d, tf), lambda i, j: (0, n_ff + j)),
            pl.BlockSpec((tf, d), lambda i, j: (j, 0)),
        ],
        out_specs=pl.BlockSpec((ROW_TILE, d), lambda i, j: (i, 0)),
        scratch_shapes=[pltpu.VMEM((ROW_TILE, d), BF16), pltpu.VMEM((ROW_TILE, d), F32)],
        compiler_params=_cparams(("parallel", "arbitrary")),
        name="ffn",
    )(x, mods, w13, w13, w2)


def _win_kernel(x_ref, mod_ref, wa_ref, wb_ref, wc_ref, wd_ref, oa_ref, ob_ref, oc_ref, od_ref):
    xm = _modulate(x_ref[...], mod_ref[3:4, :], mod_ref[4:5, :]).astype(BF16)
    oa_ref[...] = jnp.dot(xm, wa_ref[...], preferred_element_type=F32)
    ob_ref[...] = jnp.dot(xm, wb_ref[...], preferred_element_type=F32)
    oc_ref[...] = jnp.dot(xm, wc_ref[...], preferred_element_type=F32)
    od_ref[...] = jnp.dot(xm, wd_ref[...], preferred_element_type=F32)


def _win_call(x, mods, wa, wb, wc, wd, *, tiles_per_batch):
    n, d = x.shape
    widths = (wa.shape[1], wb.shape[1], wc.shape[1], wd.shape[1])
    const = lambda i: (0, 0)
    return pl.pallas_call(
        _win_kernel,
        out_shape=tuple(jax.ShapeDtypeStruct((n, w), F32) for w in widths),
        grid=(n // ROW_TILE,),
        in_specs=[
            pl.BlockSpec((ROW_TILE, d), lambda i: (i, 0)),
            pl.BlockSpec((None, N_MOD, d), lambda i: (_group_of_tile(i, tiles_per_batch), 0, 0)),
        ] + [pl.BlockSpec((d, w), const) for w in widths],
        out_specs=tuple(pl.BlockSpec((ROW_TILE, w), lambda i: (i, 0)) for w in widths),
        compiler_params=_cparams(("parallel",)),
        name="w_in",
    )(x, mods, wa, wb, wc, wd)


def _seq_flags(c, n_batch, cpb):
    j = lax.rem(jnp.maximum(c - n_batch, 0), cpb)
    is_ctx = c < n_batch
    first = jnp.logical_or(is_ctx, j == 0)
    last = jnp.logical_or(is_ctx, j == cpb - 1)
    return first, last


def _scan_chunk(b, i, n_batch, cpb, reverse):
    if reverse:
        lat = n_batch + b * cpb + (cpb - i)
    else:
        lat = n_batch + b * cpb + (i - 1)
    return jnp.where(i == 0, b, lat)


def _conv_kernel(cur_ref, prev_ref, next_ref, w_ref, b_ref, g_ref, beta_ref, o_ref, zbuf,
                 *, n_batch, cpb):
    c = pl.program_id(0)
    first, last = _seq_flags(c, n_batch, cpb)

    def glu(u):
        return u[:, :GROUP_W] * _sigmoid(u[:, GROUP_W:])

    zp = glu(prev_ref[...])
    zn = glu(next_ref[...])
    zbuf[0:CONV_HALO, :] = jnp.where(first, 0.0, zp)
    zbuf[CONV_HALO:CONV_HALO + CHUNK, :] = glu(cur_ref[...])
    zbuf[CONV_HALO + CHUNK:2 * CONV_HALO + CHUNK, :] = jnp.where(last, 0.0, zn)

    pad = CONV_K // 2
    acc = jnp.zeros((CHUNK, GROUP_W), F32)
    for j in range(CONV_K):
        start = CONV_HALO - pad + j
        acc = acc + w_ref[j:j + 1, :] * zbuf[start:start + CHUNK, :]
    acc = acc + b_ref[...]
    mu = jnp.mean(acc, axis=-1, keepdims=True)
    xc = acc - mu
    var = jnp.mean(xc * xc, axis=-1, keepdims=True)
    y = xc * lax.rsqrt(var + LN_EPS) * g_ref[...] + beta_ref[...]
    o_ref[...] = (y * _sigmoid(y)).astype(o_ref.dtype)


def _conv_call(u_a, dw_w, dw_b, ln_g, ln_b, *, n_batch, cpb):
    n = u_a.shape[0]
    n_chunks = n // CHUNK
    hpc = CHUNK // CONV_HALO
    n_halo = n // CONV_HALO
    w_pad = jnp.zeros((32, GROUP_W), F32).at[:CONV_K].set(dw_w)
    row = lambda v: v.reshape(1, GROUP_W)
    const = lambda c: (0, 0)
    kern = functools.partial(_conv_kernel, n_batch=n_batch, cpb=cpb)
    return pl.pallas_call(
        kern,
        out_shape=jax.ShapeDtypeStruct((n, GROUP_W), BF16),
        grid=(n_chunks,),
        in_specs=[
            pl.BlockSpec((CHUNK, 2 * GROUP_W), lambda c: (c, 0)),
            pl.BlockSpec((CONV_HALO, 2 * GROUP_W), lambda c: (jnp.maximum(c * hpc - 1, 0), 0)),
            pl.BlockSpec((CONV_HALO, 2 * GROUP_W),
                         lambda c: (jnp.minimum((c + 1) * hpc, n_halo - 1), 0)),
            pl.BlockSpec((32, GROUP_W), const),
            pl.BlockSpec((1, GROUP_W), const),
            pl.BlockSpec((1, GROUP_W), const),
            pl.BlockSpec((1, GROUP_W), const),
        ],
        out_specs=pl.BlockSpec((CHUNK, GROUP_W), lambda c: (c, 0)),
        scratch_shapes=[pltpu.VMEM((CHUNK + 2 * CONV_HALO, GROUP_W), F32)],
        compiler_params=_cparams(("parallel",)),
        name="conformer_conv",
    )(u_a, u_a, u_a, w_pad, row(dw_b), row(ln_g), row(ln_b))


def _lru_kernel(*refs, n_batch, cpb, reverse):
    if reverse:
        cur_ref, prev_ref, next_ref, cw_ref, cb_ref, wg_ref, bg_ref, lam_ref, o_ref, xbuf, carry = refs
        hrev_ref = None
    else:
        (cur_ref, prev_ref, next_ref, cw_ref, cb_ref, wg_ref, bg_ref, lam_ref, hrev_ref,
         o_ref, xbuf, carry) = refs
    b = pl.program_id(0)
    i = pl.program_id(1)
    c = _scan_chunk(b, i, n_batch, cpb, reverse)
    first, last = _seq_flags(c, n_batch, cpb)

    @pl.when(i == 0)
    def _():
        carry[...] = jnp.zeros_like(carry)

    u = cur_ref[...]
    gb = u[:, GROUP_W:]
    xbuf[0:SUBLANE, :] = jnp.where(first, 0.0, prev_ref[:, :GROUP_W])
    xbuf[SUBLANE:SUBLANE + CHUNK, :] = u[:, :GROUP_W]
    xbuf[SUBLANE + CHUNK:2 * SUBLANE + CHUNK, :] = jnp.where(last, 0.0, next_ref[:, :GROUP_W])
    pad_l = LRU_CONV_K // 2
    xv = jnp.zeros((CHUNK, GROUP_W), F32) + cb_ref[...]
    for j in range(LRU_CONV_K):
        start = SUBLANE - pad_l + j
        xv = xv + cw_ref[j:j + 1, :] * xbuf[start:start + CHUNK, :]

    z = _mm(xv, wg_ref[...]) + bg_ref[...]
    r = _sigmoid(z[:, :GROUP_W])
    ig = _sigmoid(z[:, GROUP_W:])
    log_a = (-LRU_C) * r * _softplus(-lam_ref[...])
    a = jnp.exp(log_a)
    t = jnp.tanh(log_a)
    bb = jnp.sqrt(-2.0 * t / (1.0 - t)) * (ig * xv)

    row = lax.broadcasted_iota(jnp.int32, (CHUNK, GROUP_W), 0)
    s = 1
    while s < CHUNK:
        if reverse:
            a_sh = pltpu.roll(a, CHUNK - s, axis=0)
            b_sh = pltpu.roll(bb, CHUNK - s, axis=0)
            valid = row < CHUNK - s
        else:
            a_sh = pltpu.roll(a, s, axis=0)
            b_sh = pltpu.roll(bb, s, axis=0)
            valid = row >= s
        bb = jnp.where(valid, a * b_sh + bb, bb)
        a = jnp.where(valid, a * a_sh, a)
        s *= 2
    h = bb + a * carry[0:1, :]
    edge = h[0:1, :] if reverse else h[CHUNK - 1:CHUNK, :]
    carry[...] = jnp.broadcast_to(edge, carry.shape)

    if reverse:
        o_ref[...] = h
    else:
        o_ref[...] = ((h + hrev_ref[...]) * _gelu_tanh(gb)).astype(o_ref.dtype)


def _lru_call(u_b, conv_w, conv_b, w_gate, b_gate, lam, h_rev, *, n_batch, cpb, reverse):
    n = u_b.shape[0]
    bpc = CHUNK // SUBLANE
    n_blk = n // SUBLANE
    chunk = lambda b, i: _scan_chunk(b, i, n_batch, cpb, reverse)
    const = lambda b, i: (0, 0)
    cw_pad = jnp.zeros((SUBLANE, GROUP_W), F32).at[:LRU_CONV_K].set(conv_w)
    in_specs = [
        pl.BlockSpec((CHUNK, 2 * GROUP_W), lambda b, i: (chunk(b, i), 0)),
        pl.BlockSpec((SUBLANE, 2 * GROUP_W), lambda b, i: (jnp.maximum(chunk(b, i) * bpc - 1, 0), 0)),
        pl.BlockSpec((SUBLANE, 2 * GROUP_W),
                     lambda b, i: (jnp.minimum((chunk(b, i) + 1) * bpc, n_blk - 1), 0)),
        pl.BlockSpec((SUBLANE, GROUP_W), const),
        pl.BlockSpec((1, GROUP_W), const),
        pl.BlockSpec((GROUP_W, 2 * GROUP_W), const),
        pl.BlockSpec((1, 2 * GROUP_W), const),
        pl.BlockSpec((1, GROUP_W), const),
    ]
    args = [u_b, u_b, u_b, cw_pad, conv_b.reshape(1, GROUP_W), w_gate, b_gate.reshape(1, 2 * GROUP_W),
            lam.reshape(1, GROUP_W)]
    if not reverse:
        in_specs.append(pl.BlockSpec((CHUNK, GROUP_W), lambda b, i: (chunk(b, i), 0)))
        args.append(h_rev)
    kern = functools.partial(_lru_kernel, n_batch=n_batch, cpb=cpb, reverse=reverse)
    return pl.pallas_call(
        kern,
        out_shape=jax.ShapeDtypeStruct((n, GROUP_W), F32 if reverse else BF16),
        grid=(n_batch, cpb + 1),
        in_specs=in_specs,
        out_specs=pl.BlockSpec((CHUNK, GROUP_W), lambda b, i: (chunk(b, i), 0)),
        scratch_shapes=[pltpu.VMEM((CHUNK + 2 * SUBLANE, GROUP_W), F32),
                        pltpu.VMEM((SUBLANE, GROUP_W), F32)],
        compiler_params=_cparams(("arbitrary", "arbitrary")),
        name="rglru_rev" if reverse else "rglru_fwd",
    )(*args)


def _rwkv_kernel(*refs, n_batch, cpb, reverse):
    if reverse:
        (cur_ref, prev_ref, next_ref, mup_ref, mun_ref, w0_ref, wup_ref, a0_ref, aup_ref, gup_ref,
         kk_ref, ka_ref, rk_ref, gng_ref, gnb_ref, hones_ref,
         y_ref, bonus_ref, ubuf, st_scr, f_lw, f_kk, f_b, f_kd, f_r, f_v) = refs
        yrev_ref = brev_ref = None
    else:
        (cur_ref, prev_ref, next_ref, mup_ref, mun_ref, w0_ref, wup_ref, a0_ref, aup_ref, gup_ref,
         kk_ref, ka_ref, rk_ref, gng_ref, gnb_ref, hones_ref, yrev_ref, brev_ref,
         y_ref, ubuf, st_scr, f_lw, f_kk, f_b, f_kd, f_r, f_v, f_y) = refs
    b = pl.program_id(0)
    i = pl.program_id(1)
    c = _scan_chunk(b, i, n_batch, cpb, reverse)
    first, last = _seq_flags(c, n_batch, cpb)
    g4 = RWKV_HEADS * RWKV_HEAD
    assert g4 == GROUP_W and RWKV_HEADS * SUB == GROUP_W

    @pl.when(i == 0)
    def _():
        st_scr[...] = jnp.zeros_like(st_scr)

    ubuf[0:SUBLANE, :] = jnp.where(first, 0.0, prev_ref[...])
    ubuf[SUBLANE:SUBLANE + CHUNK, :] = cur_ref[...]
    ubuf[SUBLANE + CHUNK:2 * SUBLANE + CHUNK, :] = jnp.where(last, 0.0, next_ref[...])
    u = cur_ref[...]
    up = ubuf[SUBLANE - 1:SUBLANE - 1 + CHUNK, :]
    un = ubuf[SUBLANE + 1:SUBLANE + 1 + CHUNK, :]
    vs = u + mup_ref[...] * (up - u) + mun_ref[...] * (un - u)

    hones = hones_ref[...]
    r = vs[:, 0:GROUP_W]
    k = vs[:, GROUP_W:2 * GROUP_W]
    val = vs[:, 2 * GROUP_W:3 * GROUP_W]
    lora_in = vs[:, 3 * GROUP_W:3 * GROUP_W + LANE]
    gate_in = vs[:, 3 * GROUP_W + LANE:]
    w_lin = _mm(jnp.tanh(lora_in), wup_ref[...])
    a_lin = _mm(lora_in, aup_ref[...])
    kq = k * kk_ref[...]
    ss = _mm_exact_rhs(kq * kq, hones)
    kk = kq * lax.rsqrt(jnp.maximum(ss, 1e-24))
    lw = -jnp.exp(-_softplus(-(w0_ref[...] + w_lin)) - 0.5)
    a = _sigmoid(a0_ref[...] + a_lin)
    kd = k * (1.0 + (a - 1.0) * ka_ref[...])
    bonus = _mm_exact_rhs(r * kd * rk_ref[...], hones) * val

    f_lw[...] = lw
    f_kk[...] = kk
    f_b[...] = kk * a
    f_kd[...] = kd
    f_r[...] = r
    f_v[...] = val

    ri = lax.broadcasted_iota(jnp.int32, (g4, g4), 0)
    ci = lax.broadcasted_iota(jnp.int32, (g4, g4), 1)
    same = (ri // SUB) == (ci // SUB)
    rt = lax.rem(ri, SUB)
    ct = lax.rem(ci, SUB)
    if reverse:
        strict = jnp.logical_and(same, ct > rt)
        incl = jnp.logical_and(same, ct >= rt)
    else:
        strict = jnp.logical_and(same, ct < rt)
        incl = jnp.logical_and(same, ct <= rt)
    eye = ri == ci
    eye_f = jnp.where(eye, 1.0, 0.0)
    ti = lax.broadcasted_iota(jnp.int32, (SUB, SUB), 0)
    tj = lax.broadcasted_iota(jnp.int32, (SUB, SUB), 1)
    tri = jnp.where((tj >= ti) if reverse else (tj <= ti), 1.0, 0.0).astype(BF16)
    lane_head = lax.broadcasted_iota(jnp.int32, (1, g4), 1) // RWKV_HEAD

    def stack(x):
        return jnp.concatenate([jnp.where(lane_head == h, x, 0.0) for h in range(RWKV_HEADS)], axis=0)

    def unstack(x):
        out = x[0:SUB, :]
        for h in range(1, RWKV_HEADS):
            out = out + x[h * SUB:(h + 1) * SUB, :]
        return out

    def sub_chunk(s, _):
        sc = (CHUNK // SUB - 1 - s) if reverse else s
        r0 = pl.multiple_of(sc * SUB, SUB)
        lw_s = f_lw[pl.ds(r0, SUB), :]
        kk_s = f_kk[pl.ds(r0, SUB), :]
        b_s = f_b[pl.ds(r0, SUB), :]
        kd_s = f_kd[pl.ds(r0, SUB), :]
        r_s = f_r[pl.ds(r0, SUB), :]
        v_s = f_v[pl.ds(r0, SUB), :]

        cum = _mm_exact_lhs(tri, lw_s)
        edge = cum[0:1, :] if reverse else cum[SUB - 1:SUB, :]
        e_in = jnp.exp(cum)
        e_ex = jnp.exp(cum - lw_s)
        e_inv = jnp.exp(-cum)
        e_out = jnp.exp(edge - cum)
        w_all = jnp.exp(edge)

        a_st = stack(e_ex * kk_s)
        b_st = stack(b_s * e_inv)
        k_st = stack(kd_s * e_inv)
        r_st = stack(r_s * e_in)
        v_st = stack(v_s)
        beta_w = b_s * e_out
        kappa_w = kd_s * e_out

        l_ab = jnp.where(strict, _mm3_nt(a_st, b_st), 0.0)
        l_ak = jnp.where(strict, _mm3_nt(a_st, k_st), 0.0)
        m_rb = jnp.where(incl, _mm3_nt(r_st, b_st), 0.0)
        m_rk = jnp.where(incl, _mm3_nt(r_st, k_st), 0.0)

        q = -l_ab
        t_inv = eye_f + q
        for _ in range(5):
            q = _mm3(q, q)
            t_inv = t_inv + _mm3(t_inv, q)

        ta_st = _mm3(t_inv, a_st)
        z0_st = _mm3(t_inv, _mm3(l_ak, v_st))
        ra_st = r_st - _mm3(m_rb, ta_st)
        y0_st = _mm3(m_rk, v_st) - _mm3(m_rb, z0_st)
        ta = unstack(ta_st)
        z0 = unstack(z0_st)

        st = st_scr[...]
        y_s = unstack(_mm3(ra_st, st) + y0_st)
        gt = jnp.where(eye, w_all, 0.0) - jnp.where(same, _mm3_tn(beta_w, ta), 0.0)
        ht = jnp.where(same, _mm3_tn(kappa_w, v_s) - _mm3_tn(beta_w, z0), 0.0)
        st_scr[...] = _mm3(gt, st) + ht
        if reverse:
            y_ref[pl.ds(r0, SUB), :] = y_s
        else:
            f_y[pl.ds(r0, SUB), :] = y_s
        return 0

    lax.fori_loop(0, CHUNK // SUB, sub_chunk, 0)

    if reverse:
        bonus_ref[...] = bonus
    else:
        yy = f_y[...] + yrev_ref[...]
        inv_n = 1.0 / RWKV_HEAD
        mu = _mm_exact_rhs(yy, hones) * inv_n
        yc = yy - mu
        var = _mm_exact_rhs(yc * yc, hones) * inv_n
        o = yc * lax.rsqrt(var + GN_EPS) * gng_ref[...] + gnb_ref[...] + bonus + brev_ref[...]
        gate = _mm(_sigmoid(gate_in), gup_ref[...])
        y_ref[...] = (o * gate).astype(y_ref.dtype)


def _rwkv_call(u_c, p, y_rev, bonus_rev, *, n_batch, cpb, reverse):
    n, wc = u_c.shape
    bpc = CHUNK // SUBLANE
    n_blk = n // SUBLANE
    chunk = lambda b, i: _scan_chunk(b, i, n_batch, cpb, reverse)
    const = lambda b, i: (0, 0)
    d = 1 if reverse else 0
    row = lambda v: v.reshape(1, -1)
    zeros64 = jnp.zeros((RWKV_HEAD, GROUP_W), F32)
    wup = jnp.concatenate([p['w_up'][d], zeros64], axis=0).astype(BF16)
    aup = jnp.concatenate([zeros64, p['a_up'][d]], axis=0).astype(BF16)
    hid = jnp.arange(GROUP_W) // RWKV_HEAD
    hones = (hid[:, None] == hid[None, :]).astype(BF16)
    small = [row(p['mu_prev']), row(p['mu_next']), row(p['w0'][d]), wup, row(p['a0'][d]), aup,
             p['g_up'].astype(BF16), row(p['k_k']), row(p['k_a']), row(p['r_k']), row(p['gn_g']),
             row(p['gn_b']), hones]
    in_specs = [
        pl.BlockSpec((CHUNK, wc), lambda b, i: (chunk(b, i), 0)),
        pl.BlockSpec((SUBLANE, wc), lambda b, i: (jnp.maximum(chunk(b, i) * bpc - 1, 0), 0)),
        pl.BlockSpec((SUBLANE, wc), lambda b, i: (jnp.minimum((chunk(b, i) + 1) * bpc, n_blk - 1), 0)),
    ] + [pl.BlockSpec(s.shape, const) for s in small]
    args = [u_c, u_c, u_c] + small
    seq_spec = pl.BlockSpec((CHUNK, GROUP_W), lambda b, i: (chunk(b, i), 0))
    scratch = [pltpu.VMEM((CHUNK + 2 * SUBLANE, wc), F32), pltpu.VMEM((GROUP_W, GROUP_W), F32)]
    scratch += [pltpu.VMEM((CHUNK, GROUP_W), F32) for _ in range(6)]
    if reverse:
        out_shape = (jax.ShapeDtypeStruct((n, GROUP_W), F32), jax.ShapeDtypeStruct((n, GROUP_W), F32))
        out_specs = (seq_spec, seq_spec)
    else:
        in_specs += [seq_spec, seq_spec]
        args += [y_rev, bonus_rev]
        out_shape = jax.ShapeDtypeStruct((n, GROUP_W), BF16)
        out_specs = seq_spec
        scratch.append(pltpu.VMEM((CHUNK, GROUP_W), F32))
    kern = functools.partial(_rwkv_kernel, n_batch=n_batch, cpb=cpb, reverse=reverse)
    return pl.pallas_call(
        kern,
        out_shape=out_shape,
        grid=(n_batch, cpb + 1),
        in_specs=in_specs,
        out_specs=out_specs,
        scratch_shapes=scratch,
        compiler_params=_cparams(("arbitrary", "arbitrary")),
        name="rwkv7_rev" if reverse else "rwkv7_fwd",
    )(*args)


def _mla_proj_kernel(u_ref, cos_ref, sin_ref, qn_ref, kvn_ref, wqm_ref, wqr_ref, wk_ref, wv_ref,
                     q_ref, k_ref, v_ref):
    u = u_ref[...]
    cq = u[:, :Q_LORA]
    ckv = u[:, Q_LORA:Q_LORA + KV_LORA]
    blk_r = u[:, Q_LORA + KV_LORA:Q_LORA + KV_LORA + LANE]
    blk_rr = u[:, Q_LORA + KV_LORA + LANE:]
    cqn = (cq * lax.rsqrt(jnp.mean(cq * cq, axis=-1, keepdims=True) + EPS) * qn_ref[...]).astype(BF16)
    ckvn = (ckv * lax.rsqrt(jnp.mean(ckv * ckv, axis=-1, keepdims=True) + EPS) * kvn_ref[...]).astype(BF16)
    cos_t = cos_ref[...]
    sin_t = sin_ref[...]
    cos4 = jnp.concatenate([cos_t] * MLA_HEADS, axis=1)
    sin4 = jnp.concatenate([sin_t] * MLA_HEADS, axis=1)
    qm = jnp.dot(cqn, wqm_ref[...], preferred_element_type=F32)
    qr = jnp.dot(cqn, wqr_ref[...], preferred_element_type=F32)
    q_ref[...] = (qm * cos4 + qr * sin4).astype(q_ref.dtype)
    kr = blk_r * cos_t + blk_rr * sin_t
    km = jnp.dot(ckvn, wk_ref[...], preferred_element_type=F32)
    k_ref[...] = (km + jnp.concatenate([kr] * MLA_HEADS, axis=1)).astype(k_ref.dtype)
    v_ref[...] = jnp.dot(ckvn, wv_ref[...], preferred_element_type=F32).astype(v_ref.dtype)


def _mla_proj_call(u_d, cos_tab, sin_tab, q_norm, kv_norm, wqm, wqr, wk, wv, *, n_batch, cpb):
    n, wd = u_d.shape
    hw = MLA_HEADS * LANE
    kv_cpb = cpb + 1

    def kv_chunk(c):
        lat = c - n_batch
        bb = lat // cpb
        return jnp.where(c < n_batch, c * kv_cpb + cpb, bb * kv_cpb + (lat - bb * cpb))

    def tab_chunk(c):
        return jnp.where(c < n_batch, 0, 1 + lax.rem(jnp.maximum(c - n_batch, 0), cpb))

    const = lambda c: (0, 0)
    return pl.pallas_call(
        _mla_proj_kernel,
        out_shape=(jax.ShapeDtypeStruct((n, hw), BF16), jax.ShapeDtypeStruct((n, hw), BF16),
                   jax.ShapeDtypeStruct((n, hw), BF16)),
        grid=(n // CHUNK,),
        in_specs=[
            pl.BlockSpec((CHUNK, wd), lambda c: (c, 0)),
            pl.BlockSpec((CHUNK, LANE), lambda c: (tab_chunk(c), 0)),
            pl.BlockSpec((CHUNK, LANE), lambda c: (tab_chunk(c), 0)),
            pl.BlockSpec((1, Q_LORA), const),
            pl.BlockSpec((1, KV_LORA), const),
            pl.BlockSpec((Q_LORA, hw), const),
            pl.BlockSpec((Q_LORA, hw), const),
            pl.BlockSpec((KV_LORA, hw), const),
            pl.BlockSpec((KV_LORA, hw), const),
        ],
        out_specs=(pl.BlockSpec((CHUNK, hw), lambda c: (c, 0)),
                   pl.BlockSpec((CHUNK, hw), lambda c: (kv_chunk(c), 0)),
                   pl.BlockSpec((CHUNK, hw), lambda c: (kv_chunk(c), 0))),
        compiler_params=_cparams(("parallel",)),
        name="mla_proj",
    )(u_d, cos_tab, sin_tab, q_norm.reshape(1, Q_LORA), kv_norm.reshape(1, KV_LORA), wqm, wqr, wk, wv)


def _attn_kernel(*refs, n_kv, aliased):
    if aliased:
        _, q_ref, k_ref, v_ref, o_ref, m_scr, l_scr, acc_scr = refs
    else:
        q_ref, k_ref, v_ref, o_ref, m_scr, l_scr, acc_scr = refs
    kj = pl.program_id(2)

    @pl.when(kj == 0)
    def _():
        m_scr[...] = jnp.full_like(m_scr, -jnp.inf)
        l_scr[...] = jnp.zeros_like(l_scr)
        acc_scr[...] = jnp.zeros_like(acc_scr)

    for h in range(MLA_HEADS):
        sl = slice(h * LANE, (h + 1) * LANE)
        s = lax.dot_general(q_ref[:, sl], k_ref[:, sl], (((1,), (1,)), ((), ())),
                            preferred_element_type=F32) * SM_SCALE
        m_prev = m_scr[h]
        m_new = jnp.maximum(m_prev, jnp.max(s, axis=-1, keepdims=True))
        p = jnp.exp(s - m_new)
        alpha = jnp.exp(m_prev - m_new)
        l_scr[h] = alpha * l_scr[h] + jnp.sum(p, axis=-1, keepdims=True)
        acc_scr[h] = alpha * acc_scr[h] + jnp.dot(p.astype(BF16), v_ref[:, sl],
                                                  preferred_element_type=F32)
        m_scr[h] = m_new

    @pl.when(kj == n_kv - 1)
    def _():
        for h in range(MLA_HEADS):
            sl = slice(h * LANE, (h + 1) * LANE)
            o_ref[:, sl] = (acc_scr[h] / l_scr[h]).astype(o_ref.dtype)


def _attn_call(q, k, v, o_prev, *, n_batch, tq, tk, n_q, n_kv, q_block, kv_block):
    n, hw = q.shape
    aliased = o_prev is not None
    in_specs = [
        pl.BlockSpec((tq, hw), lambda b, qi, kj: (q_block(b, qi), 0)),
        pl.BlockSpec((tk, hw), lambda b, qi, kj: (kv_block(b, kj), 0)),
        pl.BlockSpec((tk, hw), lambda b, qi, kj: (kv_block(b, kj), 0)),
    ]
    args = [q, k, v]
    if aliased:
        in_specs = [pl.BlockSpec(memory_space=pl.ANY)] + in_specs
        args = [o_prev] + args
    kern = functools.partial(_attn_kernel, n_kv=n_kv, aliased=aliased)
    return pl.pallas_call(
        kern,
        out_shape=jax.ShapeDtypeStruct((n, hw), BF16),
        grid=(n_batch, n_q, n_kv),
        in_specs=in_specs,
        out_specs=pl.BlockSpec((tq, hw), lambda b, qi, kj: (q_block(b, qi), 0)),
        scratch_shapes=[pltpu.VMEM((MLA_HEADS, tq, 1), F32), pltpu.VMEM((MLA_HEADS, tq, 1), F32),
                        pltpu.VMEM((MLA_HEADS, tq, LANE), F32)],
        input_output_aliases={0: 0} if aliased else {},
        compiler_params=_cparams(("parallel", "parallel", "arbitrary")),
        name="mla_attn_ctx" if aliased else "mla_attn",
    )(*args)


def _kv_tile(n_keys):
    best = LANE
    t = LANE
    while t <= 1536:
        if n_keys % t == 0:
            best = t
        t += LANE
    return best


def _wout_kernel(x_ref, mod_ref, ya_ref, yb_ref, yc_ref, yd_ref, wa_ref, wb_ref, wc_ref, wd_ref, o_ref):
    d = functools.partial(jnp.dot, preferred_element_type=F32)
    y = (d(ya_ref[...], wa_ref[...]) + d(yb_ref[...], wb_ref[...])
         + d(yc_ref[...], wc_ref[...]) + d(yd_ref[...], wd_ref[...]))
    o_ref[...] = x_ref[...] + mod_ref[5:6, :] * y


def _wout_call(x, mods, ys, ws, *, tiles_per_batch):
    n, d = x.shape
    const = lambda i: (0, 0)
    return pl.pallas_call(
        _wout_kernel,
        out_shape=jax.ShapeDtypeStruct((n, d), F32),
        grid=(n // ROW_TILE,),
        in_specs=[
            pl.BlockSpec((ROW_TILE, d), lambda i: (i, 0)),
            pl.BlockSpec((None, N_MOD, d), lambda i: (_group_of_tile(i, tiles_per_batch), 0, 0)),
        ] + [pl.BlockSpec((ROW_TILE, y.shape[1]), lambda i: (i, 0)) for y in ys]
          + [pl.BlockSpec(w.shape, const) for w in ws],
        out_specs=pl.BlockSpec((ROW_TILE, d), lambda i: (i, 0)),
        compiler_params=_cparams(("parallel",)),
        name="w_out",
    )(x, mods, *ys, *ws)


def _final_kernel(x_ref, g_ref, o_ref):
    x = x_ref[...]
    o_ref[...] = x * lax.rsqrt(jnp.mean(x * x, axis=-1, keepdims=True) + EPS) * g_ref[...]


def _final_call(x, gain, *, ctx_rows):
    n, d = x.shape
    skip = ctx_rows // ROW_TILE
    return pl.pallas_call(
        _final_kernel,
        out_shape=jax.ShapeDtypeStruct((n - ctx_rows, d), F32),
        grid=((n - ctx_rows) // ROW_TILE,),
        in_specs=[pl.BlockSpec((ROW_TILE, d), lambda i: (i + skip, 0)),
                  pl.BlockSpec((1, d), lambda i: (0, 0))],
        out_specs=pl.BlockSpec((ROW_TILE, d), lambda i: (i, 0)),
        compiler_params=_cparams(("parallel",)),
        name="final_norm",
    )(x, gain.reshape(1, d))


def _rot_cols(w):
    q = QK_ROPE // 4
    return jnp.concatenate([-w[..., q:2 * q], w[..., 0:q], -w[..., 3 * q:4 * q], w[..., 2 * q:3 * q]],
                           axis=-1)


def _block_diag(w):
    nb, n, _ = w.shape
    eye = jnp.eye(nb, dtype=w.dtype)
    return (eye[:, None, :, None] * w[:, :, None, :]).reshape(nb * n, nb * n)


def _rope_tables(t_len, ctx_len):
    t = jnp.arange(t_len, dtype=jnp.int32)
    rows = (t // GRID_W).astype(F32)
    cols = (t % GRID_W).astype(F32)
    n_freq = QK_ROPE // 4
    inv_freq = ROPE_BASE ** (-jnp.arange(n_freq, dtype=F32) / n_freq)
    ang = jnp.stack([rows[:, None] * inv_freq, cols[:, None] * inv_freq], axis=1)
    ang = jnp.concatenate([ang, ang], axis=-1).reshape(t_len, QK_ROPE)
    cos = jnp.concatenate([jnp.ones((ctx_len, QK_ROPE), F32), jnp.cos(ang)], axis=0)
    sin = jnp.concatenate([jnp.zeros((ctx_len, QK_ROPE), F32), jnp.sin(ang)], axis=0)
    n = t_len + ctx_len
    pad = LANE - QK_NOPE - QK_ROPE
    cos_tab = jnp.concatenate([jnp.ones((n, QK_NOPE), F32), cos, jnp.zeros((n, pad), F32)], axis=1)
    sin_tab = jnp.concatenate([jnp.zeros((n, QK_NOPE), F32), sin, jnp.zeros((n, pad), F32)], axis=1)
    return cos_tab, sin_tab


def kernel(x, c, ctx, c_ctx, ada_w, ada_b, ffn1_w13, ffn1_w2, ffn2_w13, ffn2_w2, w_in, w_out,
           cv_dw_w, cv_dw_b, cv_ln_g, cv_ln_b,
           lru_conv_w, lru_conv_b, lru_wa, lru_ba, lru_wx, lru_bx, lru_lambda,
           rwkv_mu_prev, rwkv_mu_next, rwkv_w0, rwkv_w_up, rwkv_a0, rwkv_a_up, rwkv_g_up,
           rwkv_k_k, rwkv_k_a, rwkv_r_k, rwkv_gn_g, rwkv_gn_b,
           mla_q_norm, mla_w_uq, mla_kv_norm, mla_w_ukv, final_norm):
    n_batch, t_len, d = x.shape
    ctx_len = ctx.shape[1]
    depth = ada_w.shape[0]
    assert ctx_len == CHUNK and n_batch * ctx_len == ROW_TILE and t_len % ROW_TILE == 0
    assert c.shape[0] + 1 <= SUBLANE
    ctx_rows = n_batch * ctx_len
    cpb = t_len // CHUNK
    tpb = t_len // ROW_TILE
    g = GROUP_W

    xs = jnp.concatenate([ctx.reshape(ctx_rows, d), x.reshape(n_batch * t_len, d)], axis=0)

    cvec = jnp.zeros((SUBLANE, d), F32).at[0].set(c_ctx).at[1:1 + n_batch].set(c)
    mods_all = _ada_call(cvec, ada_w, ada_b).reshape(depth, SUBLANE, N_MOD, d)[:, :1 + n_batch]

    cos_tab, sin_tab = _rope_tables(t_len, ctx_len)
    n_keys = t_len + ctx_len
    tk = _kv_tile(n_keys)
    tq = ROW_TILE
    zeros = jnp.zeros

    for l in range(depth):
        mods = mods_all[l]

        a_cols, b_cols, c_cols = 2 * g, 2 * g, 3 * g + 256
        wi = w_in[l]
        w_a = wi[:, :a_cols].astype(BF16)
        w_b = wi[:, a_cols:a_cols + b_cols].astype(BF16)
        w_c = wi[:, a_cols + b_cols:a_cols + b_cols + c_cols].astype(BF16)
        w_dq = wi[:, a_cols + b_cols + c_cols:]
        w_kr = w_dq[:, Q_LORA + KV_LORA:]
        z64 = zeros((d, QK_NOPE), F32)
        z32 = zeros((d, LANE - QK_NOPE - QK_ROPE), F32)
        w_d = jnp.concatenate([w_dq[:, :Q_LORA + KV_LORA], z64, w_kr, z32, z64, _rot_cols(w_kr), z32],
                              axis=1).astype(BF16)

        wq = mla_w_uq[l].reshape(Q_LORA, MLA_HEADS, QK_NOPE + QK_ROPE)
        zq = zeros((Q_LORA, MLA_HEADS, LANE - QK_NOPE - QK_ROPE), F32)
        wqm = jnp.concatenate([wq, zq], axis=-1).reshape(Q_LORA, MLA_HEADS * LANE).astype(BF16)
        wqr = jnp.concatenate([zeros((Q_LORA, MLA_HEADS, QK_NOPE), F32), _rot_cols(wq[..., QK_NOPE:]), zq],
                              axis=-1).reshape(Q_LORA, MLA_HEADS * LANE).astype(BF16)
        wkv = mla_w_ukv[l].reshape(KV_LORA, MLA_HEADS, QK_NOPE + V_HEAD)
        zk = zeros((KV_LORA, MLA_HEADS, LANE - QK_NOPE), F32)
        wk = jnp.concatenate([wkv[..., :QK_NOPE], zk], axis=-1).reshape(KV_LORA, MLA_HEADS * LANE).astype(BF16)
        wv = jnp.concatenate([wkv[..., QK_NOPE:], zeros((KV_LORA, MLA_HEADS, LANE - V_HEAD), F32)],
                             axis=-1).reshape(KV_LORA, MLA_HEADS * LANE).astype(BF16)

        wo = w_out[l]
        wo_d = wo[3 * g:].reshape(MLA_HEADS, V_HEAD, d)
        wo_d = jnp.concatenate([wo_d, zeros((MLA_HEADS, LANE - V_HEAD, d), F32)], axis=1)
        wo_parts = [wo[0:g].astype(BF16), wo[g:2 * g].astype(BF16), wo[2 * g:3 * g].astype(BF16),
                    wo_d.reshape(MLA_HEADS * LANE, d).astype(BF16)]

        lru_gate_w = [jnp.concatenate([_block_diag(lru_wa[l, dd]), _block_diag(lru_wx[l, dd])],
                                      axis=1).astype(BF16) for dd in range(2)]
        lru_gate_b = [jnp.concatenate([lru_ba[l, dd], lru_bx[l, dd]]) for dd in range(2)]
        rwkv_p = dict(mu_prev=rwkv_mu_prev[l], mu_next=rwkv_mu_next[l], w0=rwkv_w0[l], w_up=rwkv_w_up[l],
                      a0=rwkv_a0[l], a_up=rwkv_a_up[l], g_up=rwkv_g_up[l], k_k=rwkv_k_k[l], k_a=rwkv_k_a[l],
                      r_k=rwkv_r_k[l].reshape(g), gn_g=rwkv_gn_g[l], gn_b=rwkv_gn_b[l])

        xs = _ffn_call(xs, mods, ffn1_w13[l].astype(BF16), ffn1_w2[l].astype(BF16), row0=0, tiles_per_batch=tpb)

        u_a, u_b, u_c, u_d = _win_call(xs, mods, w_a, w_b, w_c, w_d, tiles_per_batch=tpb)

        y_a = _conv_call(u_a, cv_dw_w[l], cv_dw_b[l], cv_ln_g[l], cv_ln_b[l], n_batch=n_batch, cpb=cpb)

        h_rev = _lru_call(u_b, lru_conv_w[l], lru_conv_b[l], lru_gate_w[1], lru_gate_b[1], lru_lambda[l, 1],
                          None, n_batch=n_batch, cpb=cpb, reverse=True)
        y_b = _lru_call(u_b, lru_conv_w[l], lru_conv_b[l], lru_gate_w[0], lru_gate_b[0], lru_lambda[l, 0],
                        h_rev, n_batch=n_batch, cpb=cpb, reverse=False)

        y_rev, bonus_rev = _rwkv_call(u_c, rwkv_p, None, None, n_batch=n_batch, cpb=cpb, reverse=True)
        y_c = _rwkv_call(u_c, rwkv_p, y_rev, bonus_rev, n_batch=n_batch, cpb=cpb, reverse=False)

        q, k, v = _mla_proj_call(u_d, cos_tab, sin_tab, mla_q_norm[l], mla_kv_norm[l], wqm, wqr, wk, wv,
                                 n_batch=n_batch, cpb=cpb)
        q_off = ctx_rows // tq
        o_d = _attn_call(q, k, v, None, n_batch=n_batch, tq=tq, tk=tk, n_q=t_len // tq, n_kv=n_keys // tk,
                         q_block=lambda b, qi: q_off + b * (t_len // tq) + qi,
                         kv_block=lambda b, kj: b * (n_keys // tk) + kj)
        o_d = _attn_call(q, k, v, o_d, n_batch=n_batch, tq=CHUNK, tk=CHUNK, n_q=1, n_kv=1,
                         q_block=lambda b, qi: b,
                         kv_block=lambda b, kj: b * (cpb + 1) + cpb)

        xs = _wout_call(xs, mods, [y_a, y_b, y_c, o_d], wo_parts, tiles_per_batch=tpb)
        xs = _ffn_call(xs, mods, ffn2_w13[l].astype(BF16), ffn2_w2[l].astype(BF16), row0=6, tiles_per_batch=tpb)

    out = _final_call(xs, final_norm, ctx_rows=ctx_rows)
    return out.reshape(n_batch, t_len, d)
```

```python
import functools

import jax
import jax.numpy as jnp
from jax import lax
from jax.experimental import pallas as pl
from jax.experimental.pallas import tpu as pltpu

F32 = jnp.float32
BF16 = jnp.bfloat16

GRID_W = 64
N_MOD = 9
EPS = 1e-6
GROUP_W = 256
CONV_K = 31
LN_EPS = 1e-5
LRU_CONV_K = 4
LRU_C = 8.0
RWKV_HEAD = 64
RWKV_HEADS = 4
GN_EPS = 64e-5
QK_NOPE = 64
QK_ROPE = 32
V_HEAD = 64
MLA_HEADS = 4
Q_LORA = 256
KV_LORA = 128
ROPE_BASE = 10000.0
SM_SCALE = (QK_NOPE + QK_ROPE) ** -0.5

LANE = 128
SUBLANE = 8
MXU_TILE = 256
ROW_TILE = 512
CHUNK = 256
SUB = 64
CONV_HALO = 16
ATTN_MAX_KEYS = 2816
ATTN_UNIT_KEYS = 1408
VMEM_LIMIT = 48 * 1024 * 1024


def _cparams(sem):
    return pltpu.CompilerParams(dimension_semantics=sem, vmem_limit_bytes=VMEM_LIMIT)


def _sigmoid(x):
    return 1.0 / (1.0 + jnp.exp(-x))


def _softplus(x):
    return jnp.maximum(x, 0.0) + jnp.log1p(jnp.exp(-jnp.abs(x)))


def _gelu_tanh(x):
    return 0.5 * x * (1.0 + jnp.tanh(0.7978845608028654 * (x + 0.044715 * (x * x * x))))


def _mm(a, b):
    return jnp.dot(a.astype(BF16), b.astype(BF16), preferred_element_type=F32)


def _mm_nt(a, b):
    return lax.dot_general(a.astype(BF16), b.astype(BF16), (((1,), (1,)), ((), ())),
                           preferred_element_type=F32)


def _split2(a):
    hi = a.astype(BF16)
    lo = (a - hi.astype(F32)).astype(BF16)
    return hi, lo


def _split3(a):
    hi = a.astype(BF16)
    r1 = a - hi.astype(F32)
    mid = r1.astype(BF16)
    lo = (r1 - mid.astype(F32)).astype(BF16)
    return hi, mid, lo


def _mm_exact_rhs(a, b_exact):
    hi, mid, lo = _split3(a)
    d = functools.partial(jnp.dot, preferred_element_type=F32)
    return d(hi, b_exact) + d(mid, b_exact) + d(lo, b_exact)


def _mm_exact_lhs(a_exact, b):
    hi, mid, lo = _split3(b)
    d = functools.partial(jnp.dot, preferred_element_type=F32)
    return d(a_exact, hi) + d(a_exact, mid) + d(a_exact, lo)


def _mm3(a, b):
    ah, al = _split2(a)
    bh, bl = _split2(b)
    d = functools.partial(jnp.dot, preferred_element_type=F32)
    return d(ah, bh) + d(ah, bl) + d(al, bh)


def _mm3_nt(a, b):
    ah, al = _split2(a)
    bh, bl = _split2(b)
    d = functools.partial(lax.dot_general, dimension_numbers=(((1,), (1,)), ((), ())),
                          preferred_element_type=F32)
    return d(ah, bh) + d(ah, bl) + d(al, bh)


def _mm_tn(a, b):
    return lax.dot_general(a.astype(BF16), b.astype(BF16), (((0,), (0,)), ((), ())),
                           preferred_element_type=F32)


def _modulate(x, shift, scale):
    ms = jnp.mean(x * x, axis=-1, keepdims=True)
    return x * lax.rsqrt(ms + EPS) * (1.0 + scale) + shift


def _ada_kernel(c_ref, w_ref, b_ref, o_ref):
    c = c_ref[...]
    s = c * _sigmoid(c)
    o_ref[...] = _mm(s, w_ref[...]) + b_ref[...]


def _ada_call(cvec, ada_w, ada_b):
    n_layers, d, nd = ada_w.shape
    tn = nd // 8
    return pl.pallas_call(
        _ada_kernel,
        out_shape=jax.ShapeDtypeStruct((n_layers, SUBLANE, nd), F32),
        grid=(n_layers, nd // tn),
        in_specs=[
            pl.BlockSpec((SUBLANE, d), lambda l, j: (0, 0)),
            pl.BlockSpec((None, d, tn), lambda l, j: (l, 0, j)),
            pl.BlockSpec((None, 1, tn), lambda l, j: (l, 0, j)),
        ],
        out_specs=pl.BlockSpec((None, SUBLANE, tn), lambda l, j: (l, 0, j)),
        compiler_params=_cparams(("parallel", "parallel")),
        name="ada_mod",
    )(cvec, ada_w, ada_b.reshape(n_layers, 1, nd))


def _ffn_kernel(x_ref, mod_ref, w13_ref, w2_ref, o_ref, *, row0, d_ff, chunks):
    x = x_ref[...]
    xm = _modulate(x, mod_ref[row0:row0 + 1, :], mod_ref[row0 + 1:row0 + 2, :]).astype(BF16)
    acc = None
    for lo, hi in chunks:
        g = jnp.dot(xm, w13_ref[:, lo:hi], preferred_element_type=F32)
        u = jnp.dot(xm, w13_ref[:, d_ff + lo:d_ff + hi], preferred_element_type=F32)
        a = ((g * _sigmoid(g)) * u).astype(BF16)
        part = jnp.dot(a, w2_ref[lo:hi, :], preferred_element_type=F32)
        acc = part if acc is None else acc + part
    o_ref[...] = x + 0.5 * mod_ref[row0 + 2:row0 + 3, :] * acc


def _group_of_tile(i, tiles_per_batch):
    return (i + tiles_per_batch - 1) // tiles_per_batch


def _ffn_call(x, mods, w13, w2, *, row0, tiles_per_batch):
    n, d = x.shape
    d_ff = w2.shape[0]
    assert d_ff % MXU_TILE == 0
    n_tiles = d_ff // MXU_TILE
    split = ((n_tiles + 1) // 2) * MXU_TILE
    chunks = ((0, split), (split, d_ff))
    kern = functools.partial(_ffn_kernel, row0=row0, d_ff=d_ff, chunks=chunks)
    resident = pl.Buffered(1)
    return pl.pallas_call(
        kern,
        out_shape=jax.ShapeDtypeStruct((n, d), F32),
        grid=(n // ROW_TILE,),
        in_specs=[
            pl.BlockSpec((ROW_TILE, d), lambda i: (i, 0)),
            pl.BlockSpec((None, N_MOD, d), lambda i: (_group_of_tile(i, tiles_per_batch), 0, 0)),
            pl.BlockSpec((d, 2 * d_ff), lambda i: (0, 0), pipeline_mode=resident),
            pl.BlockSpec((d_ff, d), lambda i: (0, 0), pipeline_mode=resident),
        ],
        out_specs=pl.BlockSpec((ROW_TILE, d), lambda i: (i, 0)),
        compiler_params=_cparams(("parallel",)),
        name="ffn",
    )(x, mods, w13, w2)


def _win_kernel(x_ref, mod_ref, wa_ref, wb_ref, wc_ref, wd_ref, oa_ref, ob_ref, oc_ref, od_ref):
    xm = _modulate(x_ref[...], mod_ref[3:4, :], mod_ref[4:5, :]).astype(BF16)
    oa_ref[...] = jnp.dot(xm, wa_ref[...], preferred_element_type=F32)
    ob_ref[...] = jnp.dot(xm, wb_ref[...], preferred_element_type=F32)
    oc_ref[...] = jnp.dot(xm, wc_ref[...], preferred_element_type=F32)
    od_ref[...] = jnp.dot(xm, wd_ref[...], preferred_element_type=F32)


def _win_call(x, mods, wa, wb, wc, wd, *, tiles_per_batch):
    n, d = x.shape
    widths = (wa.shape[1], wb.shape[1], wc.shape[1], wd.shape[1])
    const = lambda i: (0, 0)
    return pl.pallas_call(
        _win_kernel,
        out_shape=tuple(jax.ShapeDtypeStruct((n, w), F32) for w in widths),
        grid=(n // ROW_TILE,),
        in_specs=[
            pl.BlockSpec((ROW_TILE, d), lambda i: (i, 0)),
            pl.BlockSpec((None, N_MOD, d), lambda i: (_group_of_tile(i, tiles_per_batch), 0, 0)),
        ] + [pl.BlockSpec((d, w), const) for w in widths],
        out_specs=tuple(pl.BlockSpec((ROW_TILE, w), lambda i: (i, 0)) for w in widths),
        compiler_params=_cparams(("parallel",)),
        name="w_in",
    )(x, mods, wa, wb, wc, wd)


def _seq_flags(c, n_batch, cpb):
    j = lax.rem(jnp.maximum(c - n_batch, 0), cpb)
    is_ctx = c < n_batch
    first = jnp.logical_or(is_ctx, j == 0)
    last = jnp.logical_or(is_ctx, j == cpb - 1)
    return first, last


def _scan_chunk(b, i, n_batch, cpb, reverse):
    if reverse:
        lat = n_batch + b * cpb + (cpb - i)
    else:
        lat = n_batch + b * cpb + (i - 1)
    return jnp.where(i == 0, b, lat)


def _conv_kernel(cur_ref, prev_ref, next_ref, w_ref, b_ref, g_ref, beta_ref, o_ref, zbuf,
                 *, n_batch, cpb):
    c = pl.program_id(0)
    first, last = _seq_flags(c, n_batch, cpb)

    def glu(u):
        return u[:, :GROUP_W] * _sigmoid(u[:, GROUP_W:])

    zp = glu(prev_ref[...])
    zn = glu(next_ref[...])
    zbuf[0:CONV_HALO, :] = jnp.where(first, 0.0, zp)
    zbuf[CONV_HALO:CONV_HALO + CHUNK, :] = glu(cur_ref[...])
    zbuf[CONV_HALO + CHUNK:2 * CONV_HALO + CHUNK, :] = jnp.where(last, 0.0, zn)

    pad = CONV_K // 2
    acc = jnp.zeros((CHUNK, GROUP_W), F32)
    for j in range(CONV_K):
        start = CONV_HALO - pad + j
        acc = acc + w_ref[j:j + 1, :] * zbuf[start:start + CHUNK, :]
    acc = acc + b_ref[...]
    mu = jnp.mean(acc, axis=-1, keepdims=True)
    xc = acc - mu
    var = jnp.mean(xc * xc, axis=-1, keepdims=True)
    y = xc * lax.rsqrt(var + LN_EPS) * g_ref[...] + beta_ref[...]
    o_ref[...] = (y * _sigmoid(y)).astype(o_ref.dtype)


def _conv_call(u_a, dw_w, dw_b, ln_g, ln_b, *, n_batch, cpb):
    n = u_a.shape[0]
    n_chunks = n // CHUNK
    hpc = CHUNK // CONV_HALO
    n_halo = n // CONV_HALO
    w_pad = jnp.zeros((32, GROUP_W), F32).at[:CONV_K].set(dw_w)
    row = lambda v: v.reshape(1, GROUP_W)
    const = lambda c: (0, 0)
    kern = functools.partial(_conv_kernel, n_batch=n_batch, cpb=cpb)
    return pl.pallas_call(
        kern,
        out_shape=jax.ShapeDtypeStruct((n, GROUP_W), BF16),
        grid=(n_chunks,),
        in_specs=[
            pl.BlockSpec((CHUNK, 2 * GROUP_W), lambda c: (c, 0)),
            pl.BlockSpec((CONV_HALO, 2 * GROUP_W), lambda c: (jnp.maximum(c * hpc - 1, 0), 0)),
            pl.BlockSpec((CONV_HALO, 2 * GROUP_W),
                         lambda c: (jnp.minimum((c + 1) * hpc, n_halo - 1), 0)),
            pl.BlockSpec((32, GROUP_W), const),
            pl.BlockSpec((1, GROUP_W), const),
            pl.BlockSpec((1, GROUP_W), const),
            pl.BlockSpec((1, GROUP_W), const),
        ],
        out_specs=pl.BlockSpec((CHUNK, GROUP_W), lambda c: (c, 0)),
        scratch_shapes=[pltpu.VMEM((CHUNK + 2 * CONV_HALO, GROUP_W), F32)],
        compiler_params=_cparams(("parallel",)),
        name="conformer_conv",
    )(u_a, u_a, u_a, w_pad, row(dw_b), row(ln_g), row(ln_b))


def _lru_kernel(*refs, n_batch, cpb, reverse):
    if reverse:
        cur_ref, prev_ref, next_ref, cw_ref, cb_ref, wg_ref, bg_ref, lam_ref, o_ref, xbuf, carry = refs
        hrev_ref = None
    else:
        (cur_ref, prev_ref, next_ref, cw_ref, cb_ref, wg_ref, bg_ref, lam_ref, hrev_ref,
         o_ref, xbuf, carry) = refs
    b = pl.program_id(0)
    i = pl.program_id(1)
    c = _scan_chunk(b, i, n_batch, cpb, reverse)
    first, last = _seq_flags(c, n_batch, cpb)

    @pl.when(i == 0)
    def _():
        carry[...] = jnp.zeros_like(carry)

    u = cur_ref[...]
    gb = u[:, GROUP_W:]
    xbuf[0:SUBLANE, :] = jnp.where(first, 0.0, prev_ref[:, :GROUP_W])
    xbuf[SUBLANE:SUBLANE + CHUNK, :] = u[:, :GROUP_W]
    xbuf[SUBLANE + CHUNK:2 * SUBLANE + CHUNK, :] = jnp.where(last, 0.0, next_ref[:, :GROUP_W])
    pad_l = LRU_CONV_K // 2
    xv = jnp.zeros((CHUNK, GROUP_W), F32) + cb_ref[...]
    for j in range(LRU_CONV_K):
        start = SUBLANE - pad_l + j
        xv = xv + cw_ref[j:j + 1, :] * xbuf[start:start + CHUNK, :]

    z = _mm(xv, wg_ref[...]) + bg_ref[...]
    r = _sigmoid(z[:, :GROUP_W])
    ig = _sigmoid(z[:, GROUP_W:])
    log_a = (-LRU_C) * r * _softplus(-lam_ref[...])
    a = jnp.exp(log_a)
    t = jnp.tanh(log_a)
    bb = jnp.sqrt(-2.0 * t / (1.0 - t)) * (ig * xv)

    row = lax.broadcasted_iota(jnp.int32, (CHUNK, GROUP_W), 0)
    s = 1
    while s < CHUNK:
        if reverse:
            a_sh = pltpu.roll(a, CHUNK - s, axis=0)
            b_sh = pltpu.roll(bb, CHUNK - s, axis=0)
            valid = row < CHUNK - s
        else:
            a_sh = pltpu.roll(a, s, axis=0)
            b_sh = pltpu.roll(bb, s, axis=0)
            valid = row >= s
        bb = jnp.where(valid, a * b_sh + bb, bb)
        a = jnp.where(valid, a * a_sh, a)
        s *= 2
    h = bb + a * carry[0:1, :]
    edge = h[0:1, :] if reverse else h[CHUNK - 1:CHUNK, :]
    carry[...] = jnp.broadcast_to(edge, carry.shape)

    if reverse:
        o_ref[...] = h
    else:
        o_ref[...] = ((h + hrev_ref[...]) * _gelu_tanh(gb)).astype(o_ref.dtype)


def _lru_call(u_b, conv_w, conv_b, w_gate, b_gate, lam, h_rev, *, n_batch, cpb, reverse):
    n = u_b.shape[0]
    bpc = CHUNK // SUBLANE
    n_blk = n // SUBLANE
    chunk = lambda b, i: _scan_chunk(b, i, n_batch, cpb, reverse)
    const = lambda b, i: (0, 0)
    cw_pad = jnp.zeros((SUBLANE, GROUP_W), F32).at[:LRU_CONV_K].set(conv_w)
    in_specs = [
        pl.BlockSpec((CHUNK, 2 * GROUP_W), lambda b, i: (chunk(b, i), 0)),
        pl.BlockSpec((SUBLANE, 2 * GROUP_W), lambda b, i: (jnp.maximum(chunk(b, i) * bpc - 1, 0), 0)),
        pl.BlockSpec((SUBLANE, 2 * GROUP_W),
                     lambda b, i: (jnp.minimum((chunk(b, i) + 1) * bpc, n_blk - 1), 0)),
        pl.BlockSpec((SUBLANE, GROUP_W), const),
        pl.BlockSpec((1, GROUP_W), const),
        pl.BlockSpec((GROUP_W, 2 * GROUP_W), const),
        pl.BlockSpec((1, 2 * GROUP_W), const),
        pl.BlockSpec((1, GROUP_W), const),
    ]
    args = [u_b, u_b, u_b, cw_pad, conv_b.reshape(1, GROUP_W), w_gate, b_gate.reshape(1, 2 * GROUP_W),
            lam.reshape(1, GROUP_W)]
    if not reverse:
        in_specs.append(pl.BlockSpec((CHUNK, GROUP_W), lambda b, i: (chunk(b, i), 0)))
        args.append(h_rev)
    kern = functools.partial(_lru_kernel, n_batch=n_batch, cpb=cpb, reverse=reverse)
    return pl.pallas_call(
        kern,
        out_shape=jax.ShapeDtypeStruct((n, GROUP_W), F32 if reverse else BF16),
        grid=(n_batch, cpb + 1),
        in_specs=in_specs,
        out_specs=pl.BlockSpec((CHUNK, GROUP_W), lambda b, i: (chunk(b, i), 0)),
        scratch_shapes=[pltpu.VMEM((CHUNK + 2 * SUBLANE, GROUP_W), F32),
                        pltpu.VMEM((SUBLANE, GROUP_W), F32)],
        compiler_params=_cparams(("arbitrary", "arbitrary")),
        name="rglru_rev" if reverse else "rglru_fwd",
    )(*args)


def _rwkv_kernel(*refs, n_batch, cpb, reverse):
    if reverse:
        (cur_ref, prev_ref, next_ref, mup_ref, mun_ref, w0_ref, wup_ref, a0_ref, aup_ref, gup_ref,
         kk_ref, ka_ref, rk_ref, gng_ref, gnb_ref, hones_ref,
         y_ref, bonus_ref, ubuf, st_scr) = refs
        yrev_ref = brev_ref = None
    else:
        (cur_ref, prev_ref, next_ref, mup_ref, mun_ref, w0_ref, wup_ref, a0_ref, aup_ref, gup_ref,
         kk_ref, ka_ref, rk_ref, gng_ref, gnb_ref, hones_ref, yrev_ref, brev_ref,
         y_ref, ubuf, st_scr) = refs
    b = pl.program_id(0)
    i = pl.program_id(1)
    c = _scan_chunk(b, i, n_batch, cpb, reverse)
    first, last = _seq_flags(c, n_batch, cpb)
    g4 = RWKV_HEADS * RWKV_HEAD
    assert g4 == GROUP_W and RWKV_HEADS * SUB == GROUP_W

    @pl.when(i == 0)
    def _():
        st_scr[...] = jnp.zeros_like(st_scr)

    ubuf[0:SUBLANE, :] = jnp.where(first, 0.0, prev_ref[...])
    ubuf[SUBLANE:SUBLANE + CHUNK, :] = cur_ref[...]
    ubuf[SUBLANE + CHUNK:2 * SUBLANE + CHUNK, :] = jnp.where(last, 0.0, next_ref[...])
    u = cur_ref[...]
    up = ubuf[SUBLANE - 1:SUBLANE - 1 + CHUNK, :]
    un = ubuf[SUBLANE + 1:SUBLANE + 1 + CHUNK, :]
    vs = u + mup_ref[...] * (up - u) + mun_ref[...] * (un - u)

    hones = hones_ref[...]
    r = vs[:, 0:GROUP_W]
    k = vs[:, GROUP_W:2 * GROUP_W]
    val = vs[:, 2 * GROUP_W:3 * GROUP_W]
    lora_in = vs[:, 3 * GROUP_W:3 * GROUP_W + LANE]
    gate_in = vs[:, 3 * GROUP_W + LANE:]
    w_lin = _mm(jnp.tanh(lora_in), wup_ref[...])
    a_lin = _mm(lora_in, aup_ref[...])
    kq = k * kk_ref[...]
    ss = _mm_exact_rhs(kq * kq, hones)
    kk = kq * lax.rsqrt(jnp.maximum(ss, 1e-24))
    lw = -jnp.exp(-_softplus(-(w0_ref[...] + w_lin)) - 0.5)
    a = _sigmoid(a0_ref[...] + a_lin)
    kd = k * (1.0 + (a - 1.0) * ka_ref[...])
    bonus = _mm_exact_rhs(r * kd * rk_ref[...], hones) * val

    bvec = kk * a

    ri = lax.broadcasted_iota(jnp.int32, (g4, g4), 0)
    ci = lax.broadcasted_iota(jnp.int32, (g4, g4), 1)
    same = (ri // SUB) == (ci // SUB)
    rt = lax.rem(ri, SUB)
    ct = lax.rem(ci, SUB)
    if reverse:
        strict = jnp.logical_and(same, ct > rt)
        incl = jnp.logical_and(same, ct >= rt)
    else:
        strict = jnp.logical_and(same, ct < rt)
        incl = jnp.logical_and(same, ct <= rt)
    eye = ri == ci
    ti = lax.broadcasted_iota(jnp.int32, (SUB, SUB), 0)
    tj = lax.broadcasted_iota(jnp.int32, (SUB, SUB), 1)
    tri = jnp.where((tj >= ti) if reverse else (tj <= ti), 1.0, 0.0).astype(BF16)
    lane_head = lax.broadcasted_iota(jnp.int32, (1, g4), 1) // RWKV_HEAD

    def stack(x):
        return jnp.concatenate([jnp.where(lane_head == h, x, 0.0) for h in range(RWKV_HEADS)], axis=0)

    def unstack(x):
        out = x[0:SUB, :]
        for h in range(1, RWKV_HEADS):
            out = out + x[h * SUB:(h + 1) * SUB, :]
        return out

    n_sub = CHUNK // SUB
    subs = range(n_sub)
    rows = [slice(sc * SUB, (sc + 1) * SUB) for sc in subs]
    each = lambda f, *lists: [f(*xs) for xs in zip(*lists)]
    lw_s = [lw[rw] for rw in rows]
    kk_s = [kk[rw] for rw in rows]
    b_s = [bvec[rw] for rw in rows]
    kd_s = [kd[rw] for rw in rows]
    r_s = [r[rw] for rw in rows]
    v_s = [val[rw] for rw in rows]

    cum = each(lambda x: _mm_exact_lhs(tri, x), lw_s)
    edge = each(lambda x: x[0:1, :] if reverse else x[SUB - 1:SUB, :], cum)
    a_st = each(lambda cu, l, x: stack(jnp.exp(cu - l) * x), cum, lw_s, kk_s)
    b_st = each(lambda cu, x: stack(x * jnp.exp(-cu)), cum, b_s)
    k_st = each(lambda cu, x: stack(x * jnp.exp(-cu)), cum, kd_s)
    r_st = each(lambda cu, x: stack(x * jnp.exp(cu)), cum, r_s)
    v_bf = each(lambda x: stack(x).astype(BF16), v_s)
    e_out = each(lambda ed, cu: jnp.exp(ed - cu), edge, cum)
    beta_w = each(lambda x, e: x * e, b_s, e_out)
    kappa_w = each(lambda x, e: x * e, kd_s, e_out)
    w_all = each(jnp.exp, edge)

    prod = each(lambda a_, r_, b_, k_: _mm_nt(jnp.concatenate([a_, r_], axis=0),
                                              jnp.concatenate([b_, k_], axis=0)),
                a_st, r_st, b_st, k_st)
    l_ak = each(lambda p_: jnp.where(strict, p_[0:g4, g4:], 0.0), prod)
    m_rb = each(lambda p_: jnp.where(incl, p_[g4:, 0:g4], 0.0).astype(BF16), prod)
    m_rk = each(lambda p_: jnp.where(incl, p_[g4:, g4:], 0.0), prod)

    l_ab = each(lambda p_: jnp.where(strict, p_[0:g4, 0:g4], 0.0), prod)
    q = each(lambda l_: -l_, l_ab)
    tm = q
    for _ in range(5):
        q = each(lambda q_: _mm(q_, q_), q)
        tm = each(lambda t_, q_: t_ + q_ + _mm(t_, q_), tm, q)

    def refine(t_, l_):
        res = -(t_ + l_) - _mm3(l_, t_)
        return t_ + res + _mm(t_, res)

    tm = each(refine, tm, l_ab)

    lakv = each(_mm, l_ak, v_bf)
    rhs = each(lambda a_, x: jnp.concatenate([a_, x], axis=1), a_st, lakv)
    sol = each(lambda t_, x: x + _mm(t_, x), tm, rhs)
    corr = each(_mm, m_rb, sol)
    ra_st = each(lambda r_, c_: r_ - c_[:, 0:g4], r_st, corr)
    y0_st = each(lambda m_, v_, c_: _mm(m_, v_) - c_[:, g4:], m_rk, v_bf, corr)
    ta = each(lambda s_: unstack(s_[:, 0:g4]), sol)
    z0 = each(lambda s_: unstack(s_[:, g4:]), sol)
    gt = each(lambda w_, bw, ta_: jnp.where(eye, w_, 0.0) - jnp.where(same, _mm_tn(bw, ta_), 0.0),
              w_all, beta_w, ta)
    ht = each(lambda kw, v_, bw, z_: jnp.where(same, _mm_tn(kw, v_) - _mm_tn(bw, z_), 0.0),
              kappa_w, v_s, beta_w, z0)

    order = list(range(n_sub - 1, -1, -1)) if reverse else list(range(n_sub))
    st = st_scr[...]
    y_parts = {}
    for sc in order:
        y_parts[sc] = unstack(_mm(ra_st[sc], st) + y0_st[sc])
        st = _mm3(gt[sc], st) + ht[sc]
    st_scr[...] = st
    y_all = jnp.concatenate([y_parts[sc] for sc in subs], axis=0)

    if reverse:
        y_ref[...] = y_all
        bonus_ref[...] = bonus
    else:
        yy = y_all + yrev_ref[...]
        inv_n = 1.0 / RWKV_HEAD
        mu = _mm_exact_rhs(yy, hones) * inv_n
        yc = yy - mu
        var = _mm_exact_rhs(yc * yc, hones) * inv_n
        o = yc * lax.rsqrt(var + GN_EPS) * gng_ref[...] + gnb_ref[...] + bonus + brev_ref[...]
        gate = _mm(_sigmoid(gate_in), gup_ref[...])
        y_ref[...] = (o * gate).astype(y_ref.dtype)


def _rwkv_call(u_c, p, y_rev, bonus_rev, *, n_batch, cpb, reverse):
    n, wc = u_c.shape
    bpc = CHUNK // SUBLANE
    n_blk = n // SUBLANE
    chunk = lambda b, i: _scan_chunk(b, i, n_batch, cpb, reverse)
    const = lambda b, i: (0, 0)
    d = 1 if reverse else 0
    row = lambda v: v.reshape(1, -1)
    zeros64 = jnp.zeros((RWKV_HEAD, GROUP_W), F32)
    wup = jnp.concatenate([p['w_up'][d], zeros64], axis=0).astype(BF16)
    aup = jnp.concatenate([zeros64, p['a_up'][d]], axis=0).astype(BF16)
    hid = jnp.arange(GROUP_W) // RWKV_HEAD
    hones = (hid[:, None] == hid[None, :]).astype(BF16)
    small = [row(p['mu_prev']), row(p['mu_next']), row(p['w0'][d]), wup, row(p['a0'][d]), aup,
             p['g_up'].astype(BF16), row(p['k_k']), row(p['k_a']), row(p['r_k']), row(p['gn_g']),
             row(p['gn_b']), hones]
    in_specs = [
        pl.BlockSpec((CHUNK, wc), lambda b, i: (chunk(b, i), 0)),
        pl.BlockSpec((SUBLANE, wc), lambda b, i: (jnp.maximum(chunk(b, i) * bpc - 1, 0), 0)),
        pl.BlockSpec((SUBLANE, wc), lambda b, i: (jnp.minimum((chunk(b, i) + 1) * bpc, n_blk - 1), 0)),
    ] + [pl.BlockSpec(s.shape, const) for s in small]
    args = [u_c, u_c, u_c] + small
    seq_spec = pl.BlockSpec((CHUNK, GROUP_W), lambda b, i: (chunk(b, i), 0))
    scratch = [pltpu.VMEM((CHUNK + 2 * SUBLANE, wc), F32), pltpu.VMEM((GROUP_W, GROUP_W), F32)]
    if reverse:
        out_shape = (jax.ShapeDtypeStruct((n, GROUP_W), F32), jax.ShapeDtypeStruct((n, GROUP_W), F32))
        out_specs = (seq_spec, seq_spec)
    else:
        in_specs += [seq_spec, seq_spec]
        args += [y_rev, bonus_rev]
        out_shape = jax.ShapeDtypeStruct((n, GROUP_W), BF16)
        out_specs = seq_spec
    kern = functools.partial(_rwkv_kernel, n_batch=n_batch, cpb=cpb, reverse=reverse)
    return pl.pallas_call(
        kern,
        out_shape=out_shape,
        grid=(n_batch, cpb + 1),
        in_specs=in_specs,
        out_specs=out_specs,
        scratch_shapes=scratch,
        compiler_params=_cparams(("arbitrary", "arbitrary")),
        name="rwkv7_rev" if reverse else "rwkv7_fwd",
    )(*args)


def _mla_proj_kernel(u_ref, cos_ref, sin_ref, qn_ref, kvn_ref, wqm_ref, wqr_ref, wk_ref, wv_ref,
                     q_ref, k_ref, v_ref):
    u = u_ref[...]
    cq = u[:, :Q_LORA]
    ckv = u[:, Q_LORA:Q_LORA + KV_LORA]
    blk_r = u[:, Q_LORA + KV_LORA:Q_LORA + KV_LORA + LANE]
    blk_rr = u[:, Q_LORA + KV_LORA + LANE:]
    cqn = (cq * lax.rsqrt(jnp.mean(cq * cq, axis=-1, keepdims=True) + EPS) * qn_ref[...]).astype(BF16)
    ckvn = (ckv * lax.rsqrt(jnp.mean(ckv * ckv, axis=-1, keepdims=True) + EPS) * kvn_ref[...]).astype(BF16)
    cos_t = cos_ref[...]
    sin_t = sin_ref[...]
    cos4 = jnp.concatenate([cos_t] * MLA_HEADS, axis=1)
    sin4 = jnp.concatenate([sin_t] * MLA_HEADS, axis=1)
    qm = jnp.dot(cqn, wqm_ref[...], preferred_element_type=F32)
    qr = jnp.dot(cqn, wqr_ref[...], preferred_element_type=F32)
    q_ref[...] = (qm * cos4 + qr * sin4).astype(q_ref.dtype)
    kr = blk_r * cos_t + blk_rr * sin_t
    km = jnp.dot(ckvn, wk_ref[...], preferred_element_type=F32)
    k_ref[...] = (km + jnp.concatenate([kr] * MLA_HEADS, axis=1)).astype(k_ref.dtype)
    v_ref[...] = jnp.dot(ckvn, wv_ref[...], preferred_element_type=F32).astype(v_ref.dtype)


def _mla_proj_call(u_d, cos_tab, sin_tab, q_norm, kv_norm, wqm, wqr, wk, wv, *, n_batch, cpb):
    n, wd = u_d.shape
    hw = MLA_HEADS * LANE
    kv_cpb = cpb + 1

    def kv_chunk(c):
        lat = c - n_batch
        bb = lat // cpb
        return jnp.where(c < n_batch, c * kv_cpb + cpb, bb * kv_cpb + (lat - bb * cpb))

    def tab_chunk(c):
        return jnp.where(c < n_batch, 0, 1 + lax.rem(jnp.maximum(c - n_batch, 0), cpb))

    const = lambda c: (0, 0)
    return pl.pallas_call(
        _mla_proj_kernel,
        out_shape=(jax.ShapeDtypeStruct((n, hw), BF16), jax.ShapeDtypeStruct((n, hw), BF16),
                   jax.ShapeDtypeStruct((n, hw), BF16)),
        grid=(n // CHUNK,),
        in_specs=[
            pl.BlockSpec((CHUNK, wd), lambda c: (c, 0)),
            pl.BlockSpec((CHUNK, LANE), lambda c: (tab_chunk(c), 0)),
            pl.BlockSpec((CHUNK, LANE), lambda c: (tab_chunk(c), 0)),
            pl.BlockSpec((1, Q_LORA), const),
            pl.BlockSpec((1, KV_LORA), const),
            pl.BlockSpec((Q_LORA, hw), const),
            pl.BlockSpec((Q_LORA, hw), const),
            pl.BlockSpec((KV_LORA, hw), const),
            pl.BlockSpec((KV_LORA, hw), const),
        ],
        out_specs=(pl.BlockSpec((CHUNK, hw), lambda c: (c, 0)),
                   pl.BlockSpec((CHUNK, hw), lambda c: (kv_chunk(c), 0)),
                   pl.BlockSpec((CHUNK, hw), lambda c: (kv_chunk(c), 0))),
        compiler_params=_cparams(("parallel",)),
        name="mla_proj",
    )(u_d, cos_tab, sin_tab, q_norm.reshape(1, Q_LORA), kv_norm.reshape(1, KV_LORA), wqm, wqr, wk, wv)


def _attn_kernel(*refs, n_kv, aliased, unit_keys):
    if aliased:
        _, q_ref, k_ref, v_ref, o_ref, m_scr, l_scr, acc_scr = refs
    else:
        q_ref, k_ref, v_ref, o_ref, m_scr, l_scr, acc_scr = refs
    kj = pl.program_id(2)

    @pl.when(kj == 0)
    def _():
        m_scr[...] = jnp.full_like(m_scr, -jnp.inf)
        l_scr[...] = jnp.zeros_like(l_scr)
        acc_scr[...] = jnp.zeros_like(acc_scr)

    tk = k_ref.shape[0]
    bounds = list(range(0, tk, unit_keys)) + [tk]
    units = [(h, slice(lo, hi)) for h in range(MLA_HEADS) for lo, hi in zip(bounds[:-1], bounds[1:])]

    def scores(u):
        h, keys = u
        sl = slice(h * LANE, (h + 1) * LANE)
        return lax.dot_general(q_ref[:, sl], k_ref[keys, sl], (((1,), (1,)), ((), ())),
                               preferred_element_type=F32) * SM_SCALE

    def softmax_step(u, s):
        h, _ = u
        m_prev = m_scr[h]
        m_new = jnp.maximum(m_prev, jnp.max(s, axis=-1, keepdims=True))
        p = jnp.exp(s - m_new)
        alpha = jnp.exp(m_prev - m_new)
        l_scr[h] = alpha * l_scr[h] + jnp.sum(p, axis=-1, keepdims=True)
        m_scr[h] = m_new
        return p.astype(BF16), alpha

    def weighted_values(u, p, alpha):
        h, keys = u
        sl = slice(h * LANE, (h + 1) * LANE)
        acc_scr[h] = alpha * acc_scr[h] + jnp.dot(p, v_ref[keys, sl], preferred_element_type=F32)

    s_next = scores(units[0])
    pending = None
    for idx, u in enumerate(units):
        s_cur = s_next
        if idx + 1 < len(units):
            s_next = scores(units[idx + 1])
        p, alpha = softmax_step(u, s_cur)
        if pending is not None:
            weighted_values(*pending)
        pending = (u, p, alpha)
    weighted_values(*pending)

    @pl.when(kj == n_kv - 1)
    def _():
        for h in range(MLA_HEADS):
            sl = slice(h * LANE, (h + 1) * LANE)
            o_ref[:, sl] = (acc_scr[h] / l_scr[h]).astype(o_ref.dtype)


def _attn_call(q, k, v, o_prev, *, n_batch, tq, tk, n_q, n_kv, q_block, kv_block):
    n, hw = q.shape
    aliased = o_prev is not None
    in_specs = [
        pl.BlockSpec((tq, hw), lambda b, qi, kj: (q_block(b, qi), 0)),
        pl.BlockSpec((tk, hw), lambda b, qi, kj: (kv_block(b, kj), 0)),
        pl.BlockSpec((tk, hw), lambda b, qi, kj: (kv_block(b, kj), 0)),
    ]
    args = [q, k, v]
    if aliased:
        in_specs = [pl.BlockSpec(memory_space=pl.ANY)] + in_specs
        args = [o_prev] + args
    kern = functools.partial(_attn_kernel, n_kv=n_kv, aliased=aliased, unit_keys=ATTN_UNIT_KEYS)
    return pl.pallas_call(
        kern,
        out_shape=jax.ShapeDtypeStruct((n, hw), BF16),
        grid=(n_batch, n_q, n_kv),
        in_specs=in_specs,
        out_specs=pl.BlockSpec((tq, hw), lambda b, qi, kj: (q_block(b, qi), 0)),
        scratch_shapes=[pltpu.VMEM((MLA_HEADS, tq, 1), F32), pltpu.VMEM((MLA_HEADS, tq, 1), F32),
                        pltpu.VMEM((MLA_HEADS, tq, LANE), F32)],
        input_output_aliases={0: 0} if aliased else {},
        compiler_params=_cparams(("parallel", "parallel", "arbitrary")),
        name="mla_attn_ctx" if aliased else "mla_attn",
    )(*args)


def _kv_tile(n_keys):
    best = LANE
    t = LANE
    while t <= ATTN_MAX_KEYS:
        if n_keys % t == 0:
            best = t
        t += LANE
    return best


def _wout_kernel(x_ref, mod_ref, ya_ref, yb_ref, yc_ref, yd_ref, wa_ref, wb_ref, wc_ref, wd_ref, o_ref):
    d = functools.partial(jnp.dot, preferred_element_type=F32)
    y = (d(ya_ref[...], wa_ref[...]) + d(yb_ref[...], wb_ref[...])
         + d(yc_ref[...], wc_ref[...]) + d(yd_ref[...], wd_ref[...]))
    o_ref[...] = x_ref[...] + mod_ref[5:6, :] * y


def _wout_call(x, mods, ys, ws, *, tiles_per_batch):
    n, d = x.shape
    const = lambda i: (0, 0)
    return pl.pallas_call(
        _wout_kernel,
        out_shape=jax.ShapeDtypeStruct((n, d), F32),
        grid=(n // ROW_TILE,),
        in_specs=[
            pl.BlockSpec((ROW_TILE, d), lambda i: (i, 0)),
            pl.BlockSpec((None, N_MOD, d), lambda i: (_group_of_tile(i, tiles_per_batch), 0, 0)),
        ] + [pl.BlockSpec((ROW_TILE, y.shape[1]), lambda i: (i, 0)) for y in ys]
          + [pl.BlockSpec(w.shape, const) for w in ws],
        out_specs=pl.BlockSpec((ROW_TILE, d), lambda i: (i, 0)),
        compiler_params=_cparams(("parallel",)),
        name="w_out",
    )(x, mods, *ys, *ws)


def _final_kernel(x_ref, g_ref, o_ref):
    x = x_ref[...]
    o_ref[...] = x * lax.rsqrt(jnp.mean(x * x, axis=-1, keepdims=True) + EPS) * g_ref[...]


def _final_call(x, gain, *, ctx_rows):
    n, d = x.shape
    skip = ctx_rows // ROW_TILE
    return pl.pallas_call(
        _final_kernel,
        out_shape=jax.ShapeDtypeStruct((n - ctx_rows, d), F32),
        grid=((n - ctx_rows) // ROW_TILE,),
        in_specs=[pl.BlockSpec((ROW_TILE, d), lambda i: (i + skip, 0)),
                  pl.BlockSpec((1, d), lambda i: (0, 0))],
        out_specs=pl.BlockSpec((ROW_TILE, d), lambda i: (i, 0)),
        compiler_params=_cparams(("parallel",)),
        name="final_norm",
    )(x, gain.reshape(1, d))


def _rot_cols(w):
    q = QK_ROPE // 4
    return jnp.concatenate([-w[..., q:2 * q], w[..., 0:q], -w[..., 3 * q:4 * q], w[..., 2 * q:3 * q]],
                           axis=-1)


def _block_diag(w):
    nb, n, _ = w.shape
    eye = jnp.eye(nb, dtype=w.dtype)
    return (eye[:, None, :, None] * w[:, :, None, :]).reshape(nb * n, nb * n)


def _rope_tables(t_len, ctx_len):
    t = jnp.arange(t_len, dtype=jnp.int32)
    rows = (t // GRID_W).astype(F32)
    cols = (t % GRID_W).astype(F32)
    n_freq = QK_ROPE // 4
    inv_freq = ROPE_BASE ** (-jnp.arange(n_freq, dtype=F32) / n_freq)
    ang = jnp.stack([rows[:, None] * inv_freq, cols[:, None] * inv_freq], axis=1)
    ang = jnp.concatenate([ang, ang], axis=-1).reshape(t_len, QK_ROPE)
    cos = jnp.concatenate([jnp.ones((ctx_len, QK_ROPE), F32), jnp.cos(ang)], axis=0)
    sin = jnp.concatenate([jnp.zeros((ctx_len, QK_ROPE), F32), jnp.sin(ang)], axis=0)
    n = t_len + ctx_len
    pad = LANE - QK_NOPE - QK_ROPE
    cos_tab = jnp.concatenate([jnp.ones((n, QK_NOPE), F32), cos, jnp.zeros((n, pad), F32)], axis=1)
    sin_tab = jnp.concatenate([jnp.zeros((n, QK_NOPE), F32), sin, jnp.zeros((n, pad), F32)], axis=1)
    return cos_tab, sin_tab


def kernel(x, c, ctx, c_ctx, ada_w, ada_b, ffn1_w13, ffn1_w2, ffn2_w13, ffn2_w2, w_in, w_out,
           cv_dw_w, cv_dw_b, cv_ln_g, cv_ln_b,
           lru_conv_w, lru_conv_b, lru_wa, lru_ba, lru_wx, lru_bx, lru_lambda,
           rwkv_mu_prev, rwkv_mu_next, rwkv_w0, rwkv_w_up, rwkv_a0, rwkv_a_up, rwkv_g_up,
           rwkv_k_k, rwkv_k_a, rwkv_r_k, rwkv_gn_g, rwkv_gn_b,
           mla_q_norm, mla_w_uq, mla_kv_norm, mla_w_ukv, final_norm):
    n_batch, t_len, d = x.shape
    ctx_len = ctx.shape[1]
    depth = ada_w.shape[0]
    assert ctx_len == CHUNK and n_batch * ctx_len == ROW_TILE and t_len % ROW_TILE == 0
    assert c.shape[0] + 1 <= SUBLANE
    ctx_rows = n_batch * ctx_len
    cpb = t_len // CHUNK
    tpb = t_len // ROW_TILE
    g = GROUP_W

    xs = jnp.concatenate([ctx.reshape(ctx_rows, d), x.reshape(n_batch * t_len, d)], axis=0)

    cvec = jnp.zeros((SUBLANE, d), F32).at[0].set(c_ctx).at[1:1 + n_batch].set(c)
    mods_all = _ada_call(cvec, ada_w, ada_b).reshape(depth, SUBLANE, N_MOD, d)[:, :1 + n_batch]

    cos_tab, sin_tab = _rope_tables(t_len, ctx_len)
    n_keys = t_len + ctx_len
    tk = _kv_tile(n_keys)
    tq = ROW_TILE
    zeros = jnp.zeros

    for l in range(depth):
        mods = mods_all[l]

        a_cols, b_cols, c_cols = 2 * g, 2 * g, 3 * g + 256
        wi = w_in[l]
        w_a = wi[:, :a_cols].astype(BF16)
        w_b = wi[:, a_cols:a_cols + b_cols].astype(BF16)
        w_c = wi[:, a_cols + b_cols:a_cols + b_cols + c_cols].astype(BF16)
        w_dq = wi[:, a_cols + b_cols + c_cols:]
        w_kr = w_dq[:, Q_LORA + KV_LORA:]
        z64 = zeros((d, QK_NOPE), F32)
        z32 = zeros((d, LANE - QK_NOPE - QK_ROPE), F32)
        w_d = jnp.concatenate([w_dq[:, :Q_LORA + KV_LORA], z64, w_kr, z32, z64, _rot_cols(w_kr), z32],
                              axis=1).astype(BF16)

        wq = mla_w_uq[l].reshape(Q_LORA, MLA_HEADS, QK_NOPE + QK_ROPE)
        zq = zeros((Q_LORA, MLA_HEADS, LANE - QK_NOPE - QK_ROPE), F32)
        wqm = jnp.concatenate([wq, zq], axis=-1).reshape(Q_LORA, MLA_HEADS * LANE).astype(BF16)
        wqr = jnp.concatenate([zeros((Q_LORA, MLA_HEADS, QK_NOPE), F32), _rot_cols(wq[..., QK_NOPE:]), zq],
                              axis=-1).reshape(Q_LORA, MLA_HEADS * LANE).astype(BF16)
        wkv = mla_w_ukv[l].reshape(KV_LORA, MLA_HEADS, QK_NOPE + V_HEAD)
        zk = zeros((KV_LORA, MLA_HEADS, LANE - QK_NOPE), F32)
        wk = jnp.concatenate([wkv[..., :QK_NOPE], zk], axis=-1).reshape(KV_LORA, MLA_HEADS * LANE).astype(BF16)
        wv = jnp.concatenate([wkv[..., QK_NOPE:], zeros((KV_LORA, MLA_HEADS, LANE - V_HEAD), F32)],
                             axis=-1).reshape(KV_LORA, MLA_HEADS * LANE).astype(BF16)

        wo = w_out[l]
        wo_d = wo[3 * g:].reshape(MLA_HEADS, V_HEAD, d)
        wo_d = jnp.concatenate([wo_d, zeros((MLA_HEADS, LANE - V_HEAD, d), F32)], axis=1)
        wo_parts = [wo[0:g].astype(BF16), wo[g:2 * g].astype(BF16), wo[2 * g:3 * g].astype(BF16),
                    wo_d.reshape(MLA_HEADS * LANE, d).astype(BF16)]

        lru_gate_w = [jnp.concatenate([_block_diag(lru_wa[l, dd]), _block_diag(lru_wx[l, dd])],
                                      axis=1).astype(BF16) for dd in range(2)]
        lru_gate_b = [jnp.concatenate([lru_ba[l, dd], lru_bx[l, dd]]) for dd in range(2)]
        rwkv_p = dict(mu_prev=rwkv_mu_prev[l], mu_next=rwkv_mu_next[l], w0=rwkv_w0[l], w_up=rwkv_w_up[l],
                      a0=rwkv_a0[l], a_up=rwkv_a_up[l], g_up=rwkv_g_up[l], k_k=rwkv_k_k[l], k_a=rwkv_k_a[l],
                      r_k=rwkv_r_k[l].reshape(g), gn_g=rwkv_gn_g[l], gn_b=rwkv_gn_b[l])

        xs = _ffn_call(xs, mods, ffn1_w13[l].astype(BF16), ffn1_w2[l].astype(BF16), row0=0, tiles_per_batch=tpb)

        u_a, u_b, u_c, u_d = _win_call(xs, mods, w_a, w_b, w_c, w_d, tiles_per_batch=tpb)

        y_a = _conv_call(u_a, cv_dw_w[l], cv_dw_b[l], cv_ln_g[l], cv_ln_b[l], n_batch=n_batch, cpb=cpb)

        h_rev = _lru_call(u_b, lru_conv_w[l], lru_conv_b[l], lru_gate_w[1], lru_gate_b[1], lru_lambda[l, 1],
                          None, n_batch=n_batch, cpb=cpb, reverse=True)
        y_b = _lru_call(u_b, lru_conv_w[l], lru_conv_b[l], lru_gate_w[0], lru_gate_b[0], lru_lambda[l, 0],
                        h_rev, n_batch=n_batch, cpb=cpb, reverse=False)

        y_rev, bonus_rev = _rwkv_call(u_c, rwkv_p, None, None, n_batch=n_batch, cpb=cpb, reverse=True)
        y_c = _rwkv_call(u_c, rwkv_p, y_rev, bonus_rev, n_batch=n_batch, cpb=cpb, reverse=False)

        q, k, v = _mla_proj_call(u_d, cos_tab, sin_tab, mla_q_norm[l], mla_kv_norm[l], wqm, wqr, wk, wv,
                                 n_batch=n_batch, cpb=cpb)
        q_off = ctx_rows // tq
        o_d = _attn_call(q, k, v, None, n_batch=n_batch, tq=tq, tk=tk, n_q=t_len // tq, n_kv=n_keys // tk,
                         q_block=lambda b, qi: q_off + b * (t_len // tq) + qi,
                         kv_block=lambda b, kj: b * (n_keys // tk) + kj)
        o_d = _attn_call(q, k, v, o_d, n_batch=n_batch, tq=CHUNK, tk=CHUNK, n_q=1, n_kv=1,
                         q_block=lambda b, qi: b,
                         kv_block=lambda b, kj: b * (cpb + 1) + cpb)

        xs = _wout_call(xs, mods, [y_a, y_b, y_c, o_d], wo_parts, tiles_per_batch=tpb)
        xs = _ffn_call(xs, mods, ffn2_w13[l].astype(BF16), ffn2_w2[l].astype(BF16), row0=6, tiles_per_batch=tpb)

    out = _final_call(xs, final_norm, ctx_rows=ctx_rows)
    return out.reshape(n_batch, t_len, d)
```

```python
import functools

import jax
import jax.numpy as jnp
from jax import lax
from jax.experimental import pallas as pl
from jax.experimental.pallas import tpu as pltpu

F32 = jnp.float32
BF16 = jnp.bfloat16

GRID_W = 64
N_MOD = 9
EPS = 1e-6
GROUP_W = 256
CONV_K = 31
LN_EPS = 1e-5
LRU_CONV_K = 4
LRU_C = 8.0
RWKV_HEAD = 64
RWKV_HEADS = 4
GN_EPS = 64e-5
QK_NOPE = 64
QK_ROPE = 32
V_HEAD = 64
MLA_HEADS = 4
Q_LORA = 256
KV_LORA = 128
ROPE_BASE = 10000.0
SM_SCALE = (QK_NOPE + QK_ROPE) ** -0.5
LOG2_E = 1.4426950408889634

LANE = 128
SUBLANE = 8
MXU_TILE = 256
ROW_TILE = 512
CHUNK = 256
SUB = 64
CONV_HALO = 16
ATTN_MAX_KEYS = 2816
ATTN_UNIT_KEYS = 1408
VMEM_LIMIT = 48 * 1024 * 1024


def _cparams(sem):
    return pltpu.CompilerParams(dimension_semantics=sem, vmem_limit_bytes=VMEM_LIMIT)


def _sigmoid(x):
    return 1.0 / (1.0 + jnp.exp(-x))


def _softplus(x):
    return jnp.maximum(x, 0.0) + jnp.log1p(jnp.exp(-jnp.abs(x)))


def _gelu_tanh(x):
    return 0.5 * x * (1.0 + jnp.tanh(0.7978845608028654 * (x + 0.044715 * (x * x * x))))


def _mm(a, b):
    return jnp.dot(a.astype(BF16), b.astype(BF16), preferred_element_type=F32)


def _mm_nt(a, b):
    return lax.dot_general(a.astype(BF16), b.astype(BF16), (((1,), (1,)), ((), ())),
                           preferred_element_type=F32)


def _split2(a):
    hi = a.astype(BF16)
    lo = (a - hi.astype(F32)).astype(BF16)
    return hi, lo


def _split3(a):
    hi = a.astype(BF16)
    r1 = a - hi.astype(F32)
    mid = r1.astype(BF16)
    lo = (r1 - mid.astype(F32)).astype(BF16)
    return hi, mid, lo


def _mm_exact_rhs(a, b_exact):
    hi, lo = _split2(a)
    d = functools.partial(jnp.dot, preferred_element_type=F32)
    return d(hi, b_exact) + d(lo, b_exact)


def _mm_exact_lhs(a_exact, b):
    hi, mid, lo = _split3(b)
    d = functools.partial(jnp.dot, preferred_element_type=F32)
    return d(a_exact, hi) + d(a_exact, mid) + d(a_exact, lo)


def _mm3(a, b):
    ah, al = _split2(a)
    bh, bl = _split2(b)
    d = functools.partial(jnp.dot, preferred_element_type=F32)
    return d(ah, bh) + d(ah, bl) + d(al, bh)


def _mm3_nt(a, b):
    ah, al = _split2(a)
    bh, bl = _split2(b)
    d = functools.partial(lax.dot_general, dimension_numbers=(((1,), (1,)), ((), ())),
                          preferred_element_type=F32)
    return d(ah, bh) + d(ah, bl) + d(al, bh)


def _mm_tn(a, b):
    return lax.dot_general(a.astype(BF16), b.astype(BF16), (((0,), (0,)), ((), ())),
                           preferred_element_type=F32)


def _modulate(x, shift, scale):
    ms = jnp.mean(x * x, axis=-1, keepdims=True)
    return x * lax.rsqrt(ms + EPS) * (1.0 + scale) + shift


def _ada_kernel(c_ref, w_ref, b_ref, o_ref):
    c = c_ref[...]
    s = c * _sigmoid(c)
    o_ref[...] = _mm(s, w_ref[...]) + b_ref[...]


def _ada_call(cvec, ada_w, ada_b):
    n_layers, d, nd = ada_w.shape
    tn = nd // 8
    return pl.pallas_call(
        _ada_kernel,
        out_shape=jax.ShapeDtypeStruct((n_layers, SUBLANE, nd), F32),
        grid=(n_layers, nd // tn),
        in_specs=[
            pl.BlockSpec((SUBLANE, d), lambda l, j: (0, 0)),
            pl.BlockSpec((None, d, tn), lambda l, j: (l, 0, j)),
            pl.BlockSpec((None, 1, tn), lambda l, j: (l, 0, j)),
        ],
        out_specs=pl.BlockSpec((None, SUBLANE, tn), lambda l, j: (l, 0, j)),
        compiler_params=_cparams(("parallel", "parallel")),
        name="ada_mod",
    )(cvec, ada_w, ada_b.reshape(n_layers, 1, nd))


def _ffn_kernel(*refs, row0, d_ff, chunks, split_input, mix, ctx_mix, final):
    refs = list(refs)
    o_ref = refs.pop()
    i = pl.program_id(0)
    if split_input:
        ctx_ref, lat_ref = refs.pop(0), refs.pop(0)
        x = jnp.where(i == 0, ctx_ref[...], lat_ref[...])
    else:
        x = refs.pop(0)[...]
    mod_ref = refs.pop(0)
    if mix:
        ya_ref, yb_ref, yc_ref, yd_ref = (refs.pop(0) for _ in range(4))
        yd = yd_ref[...]
        if ctx_mix:
            yd = jnp.where(i == 0, refs.pop(0)[...], yd)
        wa_ref, wb_ref, wc_ref, wd_ref = (refs.pop(0) for _ in range(4))
        d = functools.partial(jnp.dot, preferred_element_type=F32)
        y = (d(ya_ref[...], wa_ref[...]) + d(yb_ref[...], wb_ref[...])
             + d(yc_ref[...], wc_ref[...]) + d(yd, wd_ref[...]))
        x = x + mod_ref[5:6, :] * y
    w13_ref, w2_ref = refs.pop(0), refs.pop(0)
    xm = _modulate(x, mod_ref[row0:row0 + 1, :], mod_ref[row0 + 1:row0 + 2, :]).astype(BF16)
    acc = None
    for lo, hi in chunks:
        g = jnp.dot(xm, w13_ref[:, lo:hi], preferred_element_type=F32)
        u = jnp.dot(xm, w13_ref[:, d_ff + lo:d_ff + hi], preferred_element_type=F32)
        a = ((g * _sigmoid(g)) * u).astype(BF16)
        part = jnp.dot(a, w2_ref[lo:hi, :], preferred_element_type=F32)
        acc = part if acc is None else acc + part
    out = x + 0.5 * mod_ref[row0 + 2:row0 + 3, :] * acc
    if final:
        gain_ref = refs.pop(0)
        out = out * lax.rsqrt(jnp.mean(out * out, axis=-1, keepdims=True) + EPS) * gain_ref[...]
    o_ref[...] = out


def _group_of_tile(i, tiles_per_batch):
    return (i + tiles_per_batch - 1) // tiles_per_batch


def _ffn_call(x, mods, w13, w2, *, layer, row0, tiles_per_batch, ctx_rows, mix=None, final_gain=None):
    split_input = isinstance(x, tuple)
    final = final_gain is not None
    assert not (split_input and final)
    d = w13.shape[1]
    d_ff = w2.shape[1]
    n = ctx_rows + x[1].shape[0] if split_input else x.shape[0]
    skip = ctx_rows // ROW_TILE if final else 0
    assert d_ff % MXU_TILE == 0
    n_tiles = d_ff // MXU_TILE
    split = ((n_tiles + 1) // 2) * MXU_TILE
    chunks = ((0, split), (split, d_ff))
    resident = pl.Buffered(1)
    lat_tile = lambda i: jnp.maximum(i + skip - ctx_rows // ROW_TILE, 0)
    row_spec = lambda w: pl.BlockSpec((ROW_TILE, w), lambda i: (i + skip, 0))
    weight_spec = lambda w: pl.BlockSpec((None,) + w.shape[1:], lambda i: (layer, 0, 0), pipeline_mode=resident)

    in_specs, args = [], []
    if split_input:
        in_specs += [pl.BlockSpec((ROW_TILE, d), lambda i: (0, 0)),
                     pl.BlockSpec((ROW_TILE, d), lambda i: (lat_tile(i), 0))]
        args += list(x)
    else:
        in_specs.append(row_spec(d))
        args.append(x)
    in_specs.append(pl.BlockSpec((None, N_MOD, d),
                                 lambda i: (_group_of_tile(i + skip, tiles_per_batch), 0, 0)))
    args.append(mods)
    ctx_mix = False
    if mix is not None:
        ya, yb, yc, yd_lat, yd_ctx, wo_parts = mix
        in_specs += [row_spec(ya.shape[1]), row_spec(yb.shape[1]), row_spec(yc.shape[1]),
                     pl.BlockSpec((ROW_TILE, yd_lat.shape[1]), lambda i: (lat_tile(i), 0))]
        args += [ya, yb, yc, yd_lat]
        ctx_mix = yd_ctx is not None
        if ctx_mix:
            in_specs.append(pl.BlockSpec((ROW_TILE, yd_ctx.shape[1]), lambda i: (0, 0)))
            args.append(yd_ctx)
        in_specs += [weight_spec(w) for w in wo_parts]
        args += list(wo_parts)
    in_specs += [weight_spec(w13), weight_spec(w2)]
    args += [w13, w2]
    if final:
        in_specs.append(pl.BlockSpec((1, d), lambda i: (0, 0)))
        args.append(final_gain.reshape(1, d))
    kern = functools.partial(_ffn_kernel, row0=row0, d_ff=d_ff, chunks=chunks, split_input=split_input,
                             mix=mix is not None, ctx_mix=ctx_mix, final=final)
    out_rows = n - skip * ROW_TILE
    return pl.pallas_call(
        kern,
        out_shape=jax.ShapeDtypeStruct((out_rows, d), F32),
        grid=(out_rows // ROW_TILE,),
        in_specs=in_specs,
        out_specs=pl.BlockSpec((ROW_TILE, d), lambda i: (i, 0)),
        compiler_params=_cparams(("parallel",)),
        name="ffn_mix" if mix is not None else "ffn",
    )(*args)


def _win_kernel(x_ref, mod_ref, wa_ref, wb_ref, wc_ref, wd_ref, oa_ref, ob_ref, oc_ref, od_ref):
    xm = _modulate(x_ref[...], mod_ref[3:4, :], mod_ref[4:5, :]).astype(BF16)
    oa_ref[...] = jnp.dot(xm, wa_ref[...], preferred_element_type=F32)
    ob_ref[...] = jnp.dot(xm, wb_ref[...], preferred_element_type=F32)
    oc_ref[...] = jnp.dot(xm, wc_ref[...], preferred_element_type=F32)
    od_ref[...] = jnp.dot(xm, wd_ref[...], preferred_element_type=F32)


def _win_call(x, mods, wa, wb, wc, wd, *, layer, tiles_per_batch):
    n, d = x.shape
    widths = (wa.shape[2], wb.shape[2], wc.shape[2], wd.shape[2])
    return pl.pallas_call(
        _win_kernel,
        out_shape=tuple(jax.ShapeDtypeStruct((n, w), F32) for w in widths),
        grid=(n // ROW_TILE,),
        in_specs=[
            pl.BlockSpec((ROW_TILE, d), lambda i: (i, 0)),
            pl.BlockSpec((None, N_MOD, d), lambda i: (_group_of_tile(i, tiles_per_batch), 0, 0)),
        ] + [pl.BlockSpec((None, d, w), lambda i: (layer, 0, 0)) for w in widths],
        out_specs=tuple(pl.BlockSpec((ROW_TILE, w), lambda i: (i, 0)) for w in widths),
        compiler_params=_cparams(("parallel",)),
        name="w_in",
    )(x, mods, wa, wb, wc, wd)


def _seq_flags(c, n_batch, cpb):
    j = lax.rem(jnp.maximum(c - n_batch, 0), cpb)
    is_ctx = c < n_batch
    first = jnp.logical_or(is_ctx, j == 0)
    last = jnp.logical_or(is_ctx, j == cpb - 1)
    return first, last


def _scan_chunk(b, i, n_batch, cpb, reverse):
    if reverse:
        lat = n_batch + b * cpb + (cpb - i)
    else:
        lat = n_batch + b * cpb + (i - 1)
    return jnp.where(i == 0, b, lat)


def _conv_kernel(cur_ref, prev_ref, next_ref, w_ref, b_ref, g_ref, beta_ref, o_ref, zbuf, zrot,
                 *, n_batch, cpb):
    c = pl.program_id(0)
    first, last = _seq_flags(c, n_batch, cpb)

    def glu(u):
        return u[:, :GROUP_W] * _sigmoid(u[:, GROUP_W:])

    zp = glu(prev_ref[...])
    zn = glu(next_ref[...])
    zbuf[0:CONV_HALO, :] = jnp.where(first, 0.0, zp)
    zbuf[CONV_HALO:CONV_HALO + CHUNK, :] = glu(cur_ref[...])
    zbuf[CONV_HALO + CHUNK:2 * CONV_HALO + CHUNK, :] = jnp.where(last, 0.0, zn)

    pad = CONV_K // 2
    span = CHUNK + 2 * CONV_HALO - SUBLANE
    acc = jnp.zeros((CHUNK, GROUP_W), F32) + b_ref[...]
    for phase in range(SUBLANE):
        if phase:
            zrot[phase - 1] = zbuf[phase:phase + span, :]
        for j in range(CONV_K):
            start = CONV_HALO - pad + j
            if start % SUBLANE == phase:
                lo = start - phase
                win = zrot[phase - 1, lo:lo + CHUNK, :] if phase else zbuf[lo:lo + CHUNK, :]
                acc = acc + w_ref[j:j + 1, :] * win
    mu = jnp.mean(acc, axis=-1, keepdims=True)
    xc = acc - mu
    var = jnp.mean(xc * xc, axis=-1, keepdims=True)
    y = xc * lax.rsqrt(var + LN_EPS) * g_ref[...] + beta_ref[...]
    o_ref[...] = (y * _sigmoid(y)).astype(o_ref.dtype)


def _conv_call(u_a, dw_w, dw_b, ln_g, ln_b, *, n_batch, cpb):
    n = u_a.shape[0]
    n_chunks = n // CHUNK
    hpc = CHUNK // CONV_HALO
    n_halo = n // CONV_HALO
    w_pad = jnp.zeros((32, GROUP_W), F32).at[:CONV_K].set(dw_w)
    row = lambda v: v.reshape(1, GROUP_W)
    const = lambda c: (0, 0)
    kern = functools.partial(_conv_kernel, n_batch=n_batch, cpb=cpb)
    return pl.pallas_call(
        kern,
        out_shape=jax.ShapeDtypeStruct((n, GROUP_W), BF16),
        grid=(n_chunks,),
        in_specs=[
            pl.BlockSpec((CHUNK, 2 * GROUP_W), lambda c: (c, 0)),
            pl.BlockSpec((CONV_HALO, 2 * GROUP_W), lambda c: (jnp.maximum(c * hpc - 1, 0), 0)),
            pl.BlockSpec((CONV_HALO, 2 * GROUP_W),
                         lambda c: (jnp.minimum((c + 1) * hpc, n_halo - 1), 0)),
            pl.BlockSpec((32, GROUP_W), const),
            pl.BlockSpec((1, GROUP_W), const),
            pl.BlockSpec((1, GROUP_W), const),
            pl.BlockSpec((1, GROUP_W), const),
        ],
        out_specs=pl.BlockSpec((CHUNK, GROUP_W), lambda c: (c, 0)),
        scratch_shapes=[pltpu.VMEM((CHUNK + 2 * CONV_HALO, GROUP_W), F32),
                        pltpu.VMEM((SUBLANE - 1, CHUNK + 2 * CONV_HALO - SUBLANE, GROUP_W), F32)],
        compiler_params=_cparams(("parallel",)),
        name="conformer_conv",
    )(u_a, u_a, u_a, w_pad, row(dw_b), row(ln_g), row(ln_b))


def _lru_kernel(*refs, n_batch, cpb, reverse):
    if reverse:
        cur_ref, prev_ref, next_ref, cw_ref, cb_ref, wg_ref, bg_ref, lam_ref, o_ref, xbuf, carry = refs
        hrev_ref = None
    else:
        (cur_ref, prev_ref, next_ref, cw_ref, cb_ref, wg_ref, bg_ref, lam_ref, hrev_ref,
         o_ref, xbuf, carry) = refs
    b = pl.program_id(0)
    i = pl.program_id(1)
    c = _scan_chunk(b, i, n_batch, cpb, reverse)
    first, last = _seq_flags(c, n_batch, cpb)

    @pl.when(i == 0)
    def _():
        carry[...] = jnp.zeros_like(carry)

    u = cur_ref[...]
    gb = u[:, GROUP_W:]
    xbuf[0:SUBLANE, :] = jnp.where(first, 0.0, prev_ref[:, :GROUP_W])
    xbuf[SUBLANE:SUBLANE + CHUNK, :] = u[:, :GROUP_W]
    xbuf[SUBLANE + CHUNK:2 * SUBLANE + CHUNK, :] = jnp.where(last, 0.0, next_ref[:, :GROUP_W])
    pad_l = LRU_CONV_K // 2
    xv = jnp.zeros((CHUNK, GROUP_W), F32) + cb_ref[...]
    for j in range(LRU_CONV_K):
        start = SUBLANE - pad_l + j
        xv = xv + cw_ref[j:j + 1, :] * xbuf[start:start + CHUNK, :]

    z = _mm(xv, wg_ref[...]) + bg_ref[...]
    r = _sigmoid(z[:, :GROUP_W])
    ig = _sigmoid(z[:, GROUP_W:])
    log_a = (-LRU_C) * r * _softplus(-lam_ref[...])
    a = jnp.exp(log_a)
    t = jnp.tanh(log_a)
    bb = jnp.sqrt(-2.0 * t / (1.0 - t)) * (ig * xv)

    row = lax.broadcasted_iota(jnp.int32, (CHUNK, GROUP_W), 0)
    s = 1
    while s < CHUNK:
        if reverse:
            a_sh = pltpu.roll(a, CHUNK - s, axis=0)
            b_sh = pltpu.roll(bb, CHUNK - s, axis=0)
            valid = row < CHUNK - s
        else:
            a_sh = pltpu.roll(a, s, axis=0)
            b_sh = pltpu.roll(bb, s, axis=0)
            valid = row >= s
        bb = jnp.where(valid, a * b_sh + bb, bb)
        a = jnp.where(valid, a * a_sh, a)
        s *= 2
    h = bb + a * carry[0:1, :]
    edge = h[0:1, :] if reverse else h[CHUNK - 1:CHUNK, :]
    carry[...] = jnp.broadcast_to(edge, carry.shape)

    if reverse:
        o_ref[...] = h
    else:
        o_ref[...] = ((h + hrev_ref[...]) * _gelu_tanh(gb)).astype(o_ref.dtype)


def _lru_call(u_b, conv_w, conv_b, w_gate, b_gate, lam, h_rev, *, n_batch, cpb, reverse):
    n = u_b.shape[0]
    bpc = CHUNK // SUBLANE
    n_blk = n // SUBLANE
    chunk = lambda b, i: _scan_chunk(b, i, n_batch, cpb, reverse)
    const = lambda b, i: (0, 0)
    cw_pad = jnp.zeros((SUBLANE, GROUP_W), F32).at[:LRU_CONV_K].set(conv_w)
    in_specs = [
        pl.BlockSpec((CHUNK, 2 * GROUP_W), lambda b, i: (chunk(b, i), 0)),
        pl.BlockSpec((SUBLANE, 2 * GROUP_W), lambda b, i: (jnp.maximum(chunk(b, i) * bpc - 1, 0), 0)),
        pl.BlockSpec((SUBLANE, 2 * GROUP_W),
                     lambda b, i: (jnp.minimum((chunk(b, i) + 1) * bpc, n_blk - 1), 0)),
        pl.BlockSpec((SUBLANE, GROUP_W), const),
        pl.BlockSpec((1, GROUP_W), const),
        pl.BlockSpec((GROUP_W, 2 * GROUP_W), const),
        pl.BlockSpec((1, 2 * GROUP_W), const),
        pl.BlockSpec((1, GROUP_W), const),
    ]
    args = [u_b, u_b, u_b, cw_pad, conv_b.reshape(1, GROUP_W), w_gate, b_gate.reshape(1, 2 * GROUP_W),
            lam.reshape(1, GROUP_W)]
    if not reverse:
        in_specs.append(pl.BlockSpec((CHUNK, GROUP_W), lambda b, i: (chunk(b, i), 0)))
        args.append(h_rev)
    kern = functools.partial(_lru_kernel, n_batch=n_batch, cpb=cpb, reverse=reverse)
    return pl.pallas_call(
        kern,
        out_shape=jax.ShapeDtypeStruct((n, GROUP_W), F32 if reverse else BF16),
        grid=(n_batch, cpb + 1),
        in_specs=in_specs,
        out_specs=pl.BlockSpec((CHUNK, GROUP_W), lambda b, i: (chunk(b, i), 0)),
        scratch_shapes=[pltpu.VMEM((CHUNK + 2 * SUBLANE, GROUP_W), F32),
                        pltpu.VMEM((SUBLANE, GROUP_W), F32)],
        compiler_params=_cparams(("arbitrary", "arbitrary")),
        name="rglru_rev" if reverse else "rglru_fwd",
    )(*args)


def _rwkv_kernel(*refs, n_batch, cpb, reverse):
    if reverse:
        (cur_ref, prev_ref, next_ref, mup_ref, mun_ref, w0_ref, wup_ref, a0_ref, aup_ref, gup_ref,
         kk_ref, ka_ref, rk_ref, gng_ref, gnb_ref, hones_ref,
         y_ref, bonus_ref, ubuf, st_scr) = refs
        yrev_ref = brev_ref = None
    else:
        (cur_ref, prev_ref, next_ref, mup_ref, mun_ref, w0_ref, wup_ref, a0_ref, aup_ref, gup_ref,
         kk_ref, ka_ref, rk_ref, gng_ref, gnb_ref, hones_ref, yrev_ref, brev_ref,
         y_ref, ubuf, st_scr) = refs
    b = pl.program_id(0)
    i = pl.program_id(1)
    c = _scan_chunk(b, i, n_batch, cpb, reverse)
    first, last = _seq_flags(c, n_batch, cpb)
    g4 = RWKV_HEADS * RWKV_HEAD
    assert g4 == GROUP_W and RWKV_HEADS * SUB == GROUP_W

    @pl.when(i == 0)
    def _():
        st_scr[...] = jnp.zeros_like(st_scr)

    ubuf[0:SUBLANE, :] = jnp.where(first, 0.0, prev_ref[...])
    ubuf[SUBLANE:SUBLANE + CHUNK, :] = cur_ref[...]
    ubuf[SUBLANE + CHUNK:2 * SUBLANE + CHUNK, :] = jnp.where(last, 0.0, next_ref[...])
    u = cur_ref[...]
    up = ubuf[SUBLANE - 1:SUBLANE - 1 + CHUNK, :]
    un = ubuf[SUBLANE + 1:SUBLANE + 1 + CHUNK, :]
    vs = u + mup_ref[...] * (up - u) + mun_ref[...] * (un - u)

    hones = hones_ref[...]
    r = vs[:, 0:GROUP_W]
    k = vs[:, GROUP_W:2 * GROUP_W]
    val = vs[:, 2 * GROUP_W:3 * GROUP_W]
    lora_in = vs[:, 3 * GROUP_W:3 * GROUP_W + LANE]
    gate_in = vs[:, 3 * GROUP_W + LANE:]
    w_lin = _mm(jnp.tanh(lora_in), wup_ref[...])
    a_lin = _mm(lora_in, aup_ref[...])
    kq = k * kk_ref[...]
    ss = _mm_exact_rhs(kq * kq, hones)
    kk = kq * lax.rsqrt(jnp.maximum(ss, 1e-24))
    lw = -jnp.exp(-_softplus(-(w0_ref[...] + w_lin)) - 0.5)
    a = _sigmoid(a0_ref[...] + a_lin)
    kd = k * (1.0 + (a - 1.0) * ka_ref[...])
    bonus = _mm_exact_rhs(r * kd * rk_ref[...], hones) * val

    bvec = kk * a

    ri = lax.broadcasted_iota(jnp.int32, (g4, g4), 0)
    ci = lax.broadcasted_iota(jnp.int32, (g4, g4), 1)
    same = (ri // SUB) == (ci // SUB)
    rt = lax.rem(ri, SUB)
    ct = lax.rem(ci, SUB)
    if reverse:
        strict = jnp.logical_and(same, ct > rt)
        incl = jnp.logical_and(same, ct >= rt)
    else:
        strict = jnp.logical_and(same, ct < rt)
        incl = jnp.logical_and(same, ct <= rt)
    eye = ri == ci
    ti = lax.broadcasted_iota(jnp.int32, (SUB, SUB), 0)
    tj = lax.broadcasted_iota(jnp.int32, (SUB, SUB), 1)
    tri = jnp.where((tj >= ti) if reverse else (tj <= ti), 1.0, 0.0).astype(BF16)
    lane_head = lax.broadcasted_iota(jnp.int32, (1, g4), 1) // RWKV_HEAD

    def stack(x):
        return jnp.concatenate([jnp.where(lane_head == h, x, 0.0) for h in range(RWKV_HEADS)], axis=0)

    def unstack(x):
        out = x[0:SUB, :]
        for h in range(1, RWKV_HEADS):
            out = out + x[h * SUB:(h + 1) * SUB, :]
        return out

    n_sub = CHUNK // SUB
    subs = range(n_sub)
    rows = [slice(sc * SUB, (sc + 1) * SUB) for sc in subs]
    each = lambda f, *lists: [f(*xs) for xs in zip(*lists)]
    lw_s = [lw[rw] for rw in rows]
    kk_s = [kk[rw] for rw in rows]
    b_s = [bvec[rw] for rw in rows]
    kd_s = [kd[rw] for rw in rows]
    r_s = [r[rw] for rw in rows]
    v_s = [val[rw] for rw in rows]

    cum = each(lambda x: _mm_exact_lhs(tri, x), lw_s)
    edge = each(lambda x: x[0:1, :] if reverse else x[SUB - 1:SUB, :], cum)
    a_st = each(lambda cu, l, x: stack(jnp.exp(cu - l) * x), cum, lw_s, kk_s)
    b_st = each(lambda cu, x: stack(x * jnp.exp(-cu)), cum, b_s)
    k_st = each(lambda cu, x: stack(x * jnp.exp(-cu)), cum, kd_s)
    r_st = each(lambda cu, x: stack(x * jnp.exp(cu)), cum, r_s)
    v_bf = each(lambda x: stack(x).astype(BF16), v_s)
    e_out = each(lambda ed, cu: jnp.exp(ed - cu), edge, cum)
    beta_w = each(lambda x, e: x * e, b_s, e_out)
    kappa_w = each(lambda x, e: x * e, kd_s, e_out)
    w_all = each(jnp.exp, edge)

    prod = each(lambda a_, r_, b_, k_: _mm_nt(jnp.concatenate([a_, r_], axis=0),
                                              jnp.concatenate([b_, k_], axis=0)),
                a_st, r_st, b_st, k_st)
    l_ak = each(lambda p_: jnp.where(strict, p_[0:g4, g4:], 0.0), prod)
    m_rb = each(lambda p_: jnp.where(incl, p_[g4:, 0:g4], 0.0).astype(BF16), prod)
    m_rk = each(lambda p_: jnp.where(incl, p_[g4:, g4:], 0.0), prod)

    l_ab = each(lambda p_: jnp.where(strict, p_[0:g4, 0:g4], 0.0), prod)
    q = each(lambda l_: -l_, l_ab)
    tm = q
    for _ in range(5):
        q = each(lambda q_: _mm(q_, q_), q)
        tm = each(lambda t_, q_: t_ + q_ + _mm(t_, q_), tm, q)

    def refine(t_, l_):
        res = -(t_ + l_) - _mm3(l_, t_)
        return t_ + res + _mm(t_, res)

    tm = each(refine, tm, l_ab)

    lakv = each(_mm, l_ak, v_bf)
    rhs = each(lambda a_, x: jnp.concatenate([a_, x], axis=1), a_st, lakv)
    sol = each(lambda t_, x: x + _mm(t_, x), tm, rhs)
    corr = each(_mm, m_rb, sol)
    ra_st = each(lambda r_, c_: r_ - c_[:, 0:g4], r_st, corr)
    y0_st = each(lambda m_, v_, c_: _mm(m_, v_) - c_[:, g4:], m_rk, v_bf, corr)
    ta = each(lambda s_: unstack(s_[:, 0:g4]), sol)
    z0 = each(lambda s_: unstack(s_[:, g4:]), sol)
    gt = each(lambda w_, bw, ta_: jnp.where(eye, w_, 0.0) - jnp.where(same, _mm_tn(bw, ta_), 0.0),
              w_all, beta_w, ta)
    ht = each(lambda kw, v_, bw, z_: jnp.where(same, _mm_tn(kw, v_) - _mm_tn(bw, z_), 0.0),
              kappa_w, v_s, beta_w, z0)

    order = list(range(n_sub - 1, -1, -1)) if reverse else list(range(n_sub))
    st = st_scr[...]
    y_parts = {}
    for sc in order:
        y_parts[sc] = unstack(_mm(ra_st[sc], st) + y0_st[sc])
        st = _mm3(gt[sc], st) + ht[sc]
    st_scr[...] = st
    y_all = jnp.concatenate([y_parts[sc] for sc in subs], axis=0)

    if reverse:
        y_ref[...] = y_all
        bonus_ref[...] = bonus
    else:
        yy = y_all + yrev_ref[...]
        inv_n = 1.0 / RWKV_HEAD
        mu = _mm_exact_rhs(yy, hones) * inv_n
        yc = yy - mu
        var = _mm_exact_rhs(yc * yc, hones) * inv_n
        o = yc * lax.rsqrt(var + GN_EPS) * gng_ref[...] + gnb_ref[...] + bonus + brev_ref[...]
        gate = _mm(_sigmoid(gate_in), gup_ref[...])
        y_ref[...] = (o * gate).astype(y_ref.dtype)


def _rwkv_call(u_c, p, y_rev, bonus_rev, *, n_batch, cpb, reverse):
    n, wc = u_c.shape
    bpc = CHUNK // SUBLANE
    n_blk = n // SUBLANE
    chunk = lambda b, i: _scan_chunk(b, i, n_batch, cpb, reverse)
    const = lambda b, i: (0, 0)
    d = 1 if reverse else 0
    row = lambda v: v.reshape(1, -1)
    zeros64 = jnp.zeros((RWKV_HEAD, GROUP_W), F32)
    wup = jnp.concatenate([p['w_up'][d], zeros64], axis=0).astype(BF16)
    aup = jnp.concatenate([zeros64, p['a_up'][d]], axis=0).astype(BF16)
    hid = jnp.arange(GROUP_W) // RWKV_HEAD
    hones = (hid[:, None] == hid[None, :]).astype(BF16)
    small = [row(p['mu_prev']), row(p['mu_next']), row(p['w0'][d]), wup, row(p['a0'][d]), aup,
             p['g_up'].astype(BF16), row(p['k_k']), row(p['k_a']), row(p['r_k']), row(p['gn_g']),
             row(p['gn_b']), hones]
    in_specs = [
        pl.BlockSpec((CHUNK, wc), lambda b, i: (chunk(b, i), 0)),
        pl.BlockSpec((SUBLANE, wc), lambda b, i: (jnp.maximum(chunk(b, i) * bpc - 1, 0), 0)),
        pl.BlockSpec((SUBLANE, wc), lambda b, i: (jnp.minimum((chunk(b, i) + 1) * bpc, n_blk - 1), 0)),
    ] + [pl.BlockSpec(s.shape, const) for s in small]
    args = [u_c, u_c, u_c] + small
    seq_spec = pl.BlockSpec((CHUNK, GROUP_W), lambda b, i: (chunk(b, i), 0))
    scratch = [pltpu.VMEM((CHUNK + 2 * SUBLANE, wc), F32), pltpu.VMEM((GROUP_W, GROUP_W), F32)]
    if reverse:
        out_shape = (jax.ShapeDtypeStruct((n, GROUP_W), F32), jax.ShapeDtypeStruct((n, GROUP_W), F32))
        out_specs = (seq_spec, seq_spec)
    else:
        in_specs += [seq_spec, seq_spec]
        args += [y_rev, bonus_rev]
        out_shape = jax.ShapeDtypeStruct((n, GROUP_W), BF16)
        out_specs = seq_spec
    kern = functools.partial(_rwkv_kernel, n_batch=n_batch, cpb=cpb, reverse=reverse)
    return pl.pallas_call(
        kern,
        out_shape=out_shape,
        grid=(n_batch, cpb + 1),
        in_specs=in_specs,
        out_specs=out_specs,
        scratch_shapes=scratch,
        compiler_params=_cparams(("arbitrary", "arbitrary")),
        name="rwkv7_rev" if reverse else "rwkv7_fwd",
    )(*args)


def _mla_proj_kernel(u_ref, cos_ref, sin_ref, qn_ref, kvn_ref, wqm_ref, wqr_ref, wk_ref, wv_ref,
                     q_ref, k_ref, v_ref):
    u = u_ref[...]
    cq = u[:, :Q_LORA]
    ckv = u[:, Q_LORA:Q_LORA + KV_LORA]
    blk_r = u[:, Q_LORA + KV_LORA:Q_LORA + KV_LORA + LANE]
    blk_rr = u[:, Q_LORA + KV_LORA + LANE:]
    cqn = (cq * lax.rsqrt(jnp.mean(cq * cq, axis=-1, keepdims=True) + EPS) * qn_ref[...]).astype(BF16)
    ckvn = (ckv * lax.rsqrt(jnp.mean(ckv * ckv, axis=-1, keepdims=True) + EPS) * kvn_ref[...]).astype(BF16)
    cos_t = cos_ref[...]
    sin_t = sin_ref[...]
    cos4 = jnp.concatenate([cos_t] * MLA_HEADS, axis=1)
    sin4 = jnp.concatenate([sin_t] * MLA_HEADS, axis=1)
    qm = jnp.dot(cqn, wqm_ref[...], preferred_element_type=F32)
    qr = jnp.dot(cqn, wqr_ref[...], preferred_element_type=F32)
    q_ref[...] = ((qm * cos4 + qr * sin4) * (SM_SCALE * LOG2_E)).astype(q_ref.dtype)
    kr = blk_r * cos_t + blk_rr * sin_t
    km = jnp.dot(ckvn, wk_ref[...], preferred_element_type=F32)
    k_ref[...] = (km + jnp.concatenate([kr] * MLA_HEADS, axis=1)).astype(k_ref.dtype)
    v_ref[...] = jnp.dot(ckvn, wv_ref[...], preferred_element_type=F32).astype(v_ref.dtype)


def _mla_proj_call(u_d, cos_tab, sin_tab, q_norm, kv_norm, wqm, wqr, wk, wv, *, n_batch, cpb):
    n, wd = u_d.shape
    hw = MLA_HEADS * LANE
    kv_cpb = cpb + 1

    def kv_chunk(c):
        lat = c - n_batch
        bb = lat // cpb
        return jnp.where(c < n_batch, c * kv_cpb + cpb, bb * kv_cpb + (lat - bb * cpb))

    def tab_chunk(c):
        return jnp.where(c < n_batch, 0, 1 + lax.rem(jnp.maximum(c - n_batch, 0), cpb))

    const = lambda c: (0, 0)
    return pl.pallas_call(
        _mla_proj_kernel,
        out_shape=(jax.ShapeDtypeStruct((n, hw), BF16), jax.ShapeDtypeStruct((n, hw), BF16),
                   jax.ShapeDtypeStruct((n, hw), BF16)),
        grid=(n // CHUNK,),
        in_specs=[
            pl.BlockSpec((CHUNK, wd), lambda c: (c, 0)),
            pl.BlockSpec((CHUNK, LANE), lambda c: (tab_chunk(c), 0)),
            pl.BlockSpec((CHUNK, LANE), lambda c: (tab_chunk(c), 0)),
            pl.BlockSpec((1, Q_LORA), const),
            pl.BlockSpec((1, KV_LORA), const),
            pl.BlockSpec((Q_LORA, hw), const),
            pl.BlockSpec((Q_LORA, hw), const),
            pl.BlockSpec((KV_LORA, hw), const),
            pl.BlockSpec((KV_LORA, hw), const),
        ],
        out_specs=(pl.BlockSpec((CHUNK, hw), lambda c: (c, 0)),
                   pl.BlockSpec((CHUNK, hw), lambda c: (kv_chunk(c), 0)),
                   pl.BlockSpec((CHUNK, hw), lambda c: (kv_chunk(c), 0))),
        compiler_params=_cparams(("parallel",)),
        name="mla_proj",
    )(u_d, cos_tab, sin_tab, q_norm.reshape(1, Q_LORA), kv_norm.reshape(1, KV_LORA), wqm, wqr, wk, wv)


def _attn_kernel(q_ref, k_ref, v_ref, o_ref, m_scr, l_scr, acc_scr, *, n_kv, unit_keys):
    kj = pl.program_id(2)

    @pl.when(kj == 0)
    def _():
        m_scr[...] = jnp.full_like(m_scr, -jnp.inf)
        l_scr[...] = jnp.zeros_like(l_scr)
        acc_scr[...] = jnp.zeros_like(acc_scr)

    tk = k_ref.shape[0]
    bounds = list(range(0, tk, unit_keys)) + [tk]
    units = [(h, lo, hi) for h in range(MLA_HEADS) for lo, hi in zip(bounds[:-1], bounds[1:])]

    def scores(u):
        h, lo, hi = u
        sl = slice(h * LANE, (h + 1) * LANE)
        return lax.dot_general(q_ref[:, sl], k_ref[lo:hi, sl], (((1,), (1,)), ((), ())),
                               preferred_element_type=F32)

    def softmax_step(u, s):
        h = u[0]
        m_prev = m_scr[h]
        m_new = jnp.maximum(m_prev, jnp.max(s, axis=-1, keepdims=True))
        p = jnp.exp2(s - m_new)
        alpha = jnp.exp2(m_prev - m_new)
        l_scr[h] = alpha * l_scr[h] + jnp.sum(p, axis=-1, keepdims=True)
        m_scr[h] = m_new
        return p.astype(BF16), alpha

    def weighted_values(u, p, alpha):
        h, lo, hi = u
        sl = slice(h * LANE, (h + 1) * LANE)
        acc_scr[h] = alpha * acc_scr[h] + jnp.dot(p, v_ref[lo:hi, sl], preferred_element_type=F32)

    s_next = scores(units[0])
    pending = None
    for idx, u in enumerate(units):
        s_cur = s_next
        if idx + 1 < len(units):
            s_next = scores(units[idx + 1])
        p, alpha = softmax_step(u, s_cur)
        if pending is not None:
            weighted_values(*pending)
        pending = (u, p, alpha)
    weighted_values(*pending)

    @pl.when(kj == n_kv - 1)
    def _():
        for h in range(MLA_HEADS):
            sl = slice(h * LANE, (h + 1) * LANE)
            o_ref[:, sl] = (acc_scr[h] / l_scr[h]).astype(o_ref.dtype)


def _attn_call(q, k, v, *, out_rows, n_batch, tq, tk, n_q, n_kv, q_block, kv_block, o_block, name):
    hw = q.shape[1]
    unit = min(ATTN_UNIT_KEYS, tk)
    kern = functools.partial(_attn_kernel, n_kv=n_kv, unit_keys=unit)
    return pl.pallas_call(
        kern,
        out_shape=jax.ShapeDtypeStruct((out_rows, hw), BF16),
        grid=(n_batch, n_q, n_kv),
        in_specs=[
            pl.BlockSpec((tq, hw), lambda b, qi, kj: (q_block(b, qi), 0)),
            pl.BlockSpec((tk, hw), lambda b, qi, kj: (kv_block(b, kj), 0)),
            pl.BlockSpec((tk, hw), lambda b, qi, kj: (kv_block(b, kj), 0)),
        ],
        out_specs=pl.BlockSpec((tq, hw), lambda b, qi, kj: (o_block(b, qi), 0)),
        scratch_shapes=[pltpu.VMEM((MLA_HEADS, tq, 1), F32), pltpu.VMEM((MLA_HEADS, tq, 1), F32),
                        pltpu.VMEM((MLA_HEADS, tq, LANE), F32)],
        compiler_params=_cparams(("parallel", "parallel", "arbitrary")),
        name=name,
    )(q, k, v)


def _kv_tile(n_keys):
    best = LANE
    t = LANE
    while t <= ATTN_MAX_KEYS:
        if n_keys % t == 0:
            best = t
        t += LANE
    return best


def _rot_cols(w):
    q = QK_ROPE // 4
    return jnp.concatenate([-w[..., q:2 * q], w[..., 0:q], -w[..., 3 * q:4 * q], w[..., 2 * q:3 * q]],
                           axis=-1)


def _block_diag(w):
    nb, n = w.shape[-3], w.shape[-2]
    eye = jnp.eye(nb, dtype=w.dtype)
    out = eye[:, None, :, None] * w[..., :, :, None, :]
    return out.reshape(w.shape[:-3] + (nb * n, nb * n))


def _rope_tables(t_len, ctx_len):
    t = jnp.arange(t_len, dtype=jnp.int32)
    rows = (t // GRID_W).astype(F32)
    cols = (t % GRID_W).astype(F32)
    n_freq = QK_ROPE // 4
    inv_freq = ROPE_BASE ** (-jnp.arange(n_freq, dtype=F32) / n_freq)
    ang = jnp.stack([rows[:, None] * inv_freq, cols[:, None] * inv_freq], axis=1)
    ang = jnp.concatenate([ang, ang], axis=-1).reshape(t_len, QK_ROPE)
    cos = jnp.concatenate([jnp.ones((ctx_len, QK_ROPE), F32), jnp.cos(ang)], axis=0)
    sin = jnp.concatenate([jnp.zeros((ctx_len, QK_ROPE), F32), jnp.sin(ang)], axis=0)
    n = t_len + ctx_len
    pad = LANE - QK_NOPE - QK_ROPE
    cos_tab = jnp.concatenate([jnp.ones((n, QK_NOPE), F32), cos, jnp.zeros((n, pad), F32)], axis=1)
    sin_tab = jnp.concatenate([jnp.zeros((n, QK_NOPE), F32), sin, jnp.zeros((n, pad), F32)], axis=1)
    return cos_tab, sin_tab


def kernel(x, c, ctx, c_ctx, ada_w, ada_b, ffn1_w13, ffn1_w2, ffn2_w13, ffn2_w2, w_in, w_out,
           cv_dw_w, cv_dw_b, cv_ln_g, cv_ln_b,
           lru_conv_w, lru_conv_b, lru_wa, lru_ba, lru_wx, lru_bx, lru_lambda,
           rwkv_mu_prev, rwkv_mu_next, rwkv_w0, rwkv_w_up, rwkv_a0, rwkv_a_up, rwkv_g_up,
           rwkv_k_k, rwkv_k_a, rwkv_r_k, rwkv_gn_g, rwkv_gn_b,
           mla_q_norm, mla_w_uq, mla_kv_norm, mla_w_ukv, final_norm):
    n_batch, t_len, d = x.shape
    ctx_len = ctx.shape[1]
    depth = ada_w.shape[0]
    assert ctx_len == CHUNK and n_batch * ctx_len == ROW_TILE and t_len % ROW_TILE == 0
    assert c.shape[0] + 1 <= SUBLANE
    ctx_rows = n_batch * ctx_len
    cpb = t_len // CHUNK
    tpb = t_len // ROW_TILE
    g = GROUP_W

    cvec = jnp.concatenate([c_ctx[None, :], c, jnp.zeros((SUBLANE - 1 - n_batch, d), F32)], axis=0)
    mods_all = _ada_call(cvec, ada_w, ada_b).reshape(depth, SUBLANE, N_MOD, d)

    cos_tab, sin_tab = _rope_tables(t_len, ctx_len)
    n_keys = t_len + ctx_len
    tk = _kv_tile(n_keys)
    tq = ROW_TILE
    zeros = jnp.zeros

    a_cols, b_cols, c_cols = 2 * g, 2 * g, 3 * g + 256
    w_a = w_in[:, :, :a_cols].astype(BF16)
    w_b = w_in[:, :, a_cols:a_cols + b_cols].astype(BF16)
    w_c = w_in[:, :, a_cols + b_cols:a_cols + b_cols + c_cols].astype(BF16)
    w_dq = w_in[:, :, a_cols + b_cols + c_cols:]
    w_kr = w_dq[:, :, Q_LORA + KV_LORA:]
    z64 = zeros((depth, d, QK_NOPE), F32)
    z32 = zeros((depth, d, LANE - QK_NOPE - QK_ROPE), F32)
    w_d = jnp.concatenate([w_dq[:, :, :Q_LORA + KV_LORA], z64, w_kr, z32, z64, _rot_cols(w_kr), z32],
                          axis=2).astype(BF16)

    hw = MLA_HEADS * LANE
    wq = mla_w_uq.reshape(depth, Q_LORA, MLA_HEADS, QK_NOPE + QK_ROPE)
    zq = zeros((depth, Q_LORA, MLA_HEADS, LANE - QK_NOPE - QK_ROPE), F32)
    wqm = jnp.concatenate([wq, zq], axis=-1).reshape(depth, Q_LORA, hw).astype(BF16)
    wqr = jnp.concatenate([zeros((depth, Q_LORA, MLA_HEADS, QK_NOPE), F32), _rot_cols(wq[..., QK_NOPE:]), zq],
                          axis=-1).reshape(depth, Q_LORA, hw).astype(BF16)
    wkv = mla_w_ukv.reshape(depth, KV_LORA, MLA_HEADS, QK_NOPE + V_HEAD)
    zk = zeros((depth, KV_LORA, MLA_HEADS, LANE - QK_NOPE), F32)
    wk = jnp.concatenate([wkv[..., :QK_NOPE], zk], axis=-1).reshape(depth, KV_LORA, hw).astype(BF16)
    wv = jnp.concatenate([wkv[..., QK_NOPE:], zk], axis=-1).reshape(depth, KV_LORA, hw).astype(BF16)

    wo_d = w_out[:, 3 * g:].reshape(depth, MLA_HEADS, V_HEAD, d)
    wo_d = jnp.concatenate([wo_d, zeros((depth, MLA_HEADS, LANE - V_HEAD, d), F32)], axis=2)
    wo_parts = [w_out[:, 0:g].astype(BF16), w_out[:, g:2 * g].astype(BF16), w_out[:, 2 * g:3 * g].astype(BF16),
                wo_d.reshape(depth, hw, d).astype(BF16)]

    lru_gate_w = jnp.concatenate([_block_diag(lru_wa), _block_diag(lru_wx)], axis=-1).astype(BF16)
    lru_gate_b = jnp.concatenate([lru_ba, lru_bx], axis=-1)
    ffn_w = [(ffn1_w13.astype(BF16), ffn1_w2.astype(BF16)), (ffn2_w13.astype(BF16), ffn2_w2.astype(BF16))]

    ctx_flat = ctx.reshape(ctx_rows, d)
    x_flat = x.reshape(n_batch * t_len, d)
    xs = None
    out = None
    for l in range(depth):
        mods = mods_all[l]
        last = l == depth - 1
        rwkv_p = dict(mu_prev=rwkv_mu_prev[l], mu_next=rwkv_mu_next[l], w0=rwkv_w0[l], w_up=rwkv_w_up[l],
                      a0=rwkv_a0[l], a_up=rwkv_a_up[l], g_up=rwkv_g_up[l], k_k=rwkv_k_k[l], k_a=rwkv_k_a[l],
                      r_k=rwkv_r_k[l].reshape(g), gn_g=rwkv_gn_g[l], gn_b=rwkv_gn_b[l])

        xs = _ffn_call((ctx_flat, x_flat) if l == 0 else xs, mods, *ffn_w[0], layer=l, row0=0,
                       tiles_per_batch=tpb, ctx_rows=ctx_rows)

        u_a, u_b, u_c, u_d = _win_call(xs, mods, w_a, w_b, w_c, w_d, layer=l, tiles_per_batch=tpb)

        y_a = _conv_call(u_a, cv_dw_w[l], cv_dw_b[l], cv_ln_g[l], cv_ln_b[l], n_batch=n_batch, cpb=cpb)

        h_rev = _lru_call(u_b, lru_conv_w[l], lru_conv_b[l], lru_gate_w[l, 1], lru_gate_b[l, 1], lru_lambda[l, 1],
                          None, n_batch=n_batch, cpb=cpb, reverse=True)
        y_b = _lru_call(u_b, lru_conv_w[l], lru_conv_b[l], lru_gate_w[l, 0], lru_gate_b[l, 0], lru_lambda[l, 0],
                        h_rev, n_batch=n_batch, cpb=cpb, reverse=False)

        y_rev, bonus_rev = _rwkv_call(u_c, rwkv_p, None, None, n_batch=n_batch, cpb=cpb, reverse=True)
        y_c = _rwkv_call(u_c, rwkv_p, y_rev, bonus_rev, n_batch=n_batch, cpb=cpb, reverse=False)

        q, k, v = _mla_proj_call(u_d, cos_tab, sin_tab, mla_q_norm[l], mla_kv_norm[l], wqm[l], wqr[l], wk[l],
                                 wv[l], n_batch=n_batch, cpb=cpb)
        q_off = ctx_rows // tq
        o_lat = _attn_call(q, k, v, out_rows=n_batch * t_len, n_batch=n_batch, tq=tq, tk=tk,
                           n_q=t_len // tq, n_kv=n_keys // tk,
                           q_block=lambda b, qi: q_off + b * (t_len // tq) + qi,
                           kv_block=lambda b, kj: b * (n_keys // tk) + kj,
                           o_block=lambda b, qi: b * (t_len // tq) + qi, name="mla_attn")
        o_ctx = None if last else _attn_call(
            q, k, v, out_rows=ctx_rows, n_batch=n_batch, tq=CHUNK, tk=CHUNK, n_q=1, n_kv=1,
            q_block=lambda b, qi: b, kv_block=lambda b, kj: b * (cpb + 1) + cpb,
            o_block=lambda b, qi: b, name="mla_attn_ctx")

        mix = (y_a, y_b, y_c, o_lat, o_ctx, wo_parts)
        if last:
            out = _ffn_call(xs, mods, *ffn_w[1], layer=l, row0=6, tiles_per_batch=tpb, ctx_rows=ctx_rows,
                            mix=mix, final_gain=final_norm)
        else:
            xs = _ffn_call(xs, mods, *ffn_w[1], layer=l, row0=6, tiles_per_batch=tpb, ctx_rows=ctx_rows, mix=mix)

    return out.reshape(n_batch, t_len, d)
```

```python
import functools

import jax
import jax.numpy as jnp
from jax import lax
from jax.experimental import pallas as pl
from jax.experimental.pallas import tpu as pltpu

F32 = jnp.float32
BF16 = jnp.bfloat16

GRID_W = 64
N_MOD = 9
EPS = 1e-6
GROUP_W = 256
CONV_K = 31
LN_EPS = 1e-5
LRU_CONV_K = 4
LRU_C = 8.0
RWKV_HEAD = 64
RWKV_HEADS = 4
GN_EPS = 64e-5
QK_NOPE = 64
QK_ROPE = 32
V_HEAD = 64
MLA_HEADS = 4
Q_LORA = 256
KV_LORA = 128
ROPE_BASE = 10000.0
SM_SCALE = (QK_NOPE + QK_ROPE) ** -0.5
LOG2_E = 1.4426950408889634

LANE = 128
SUBLANE = 8
MXU_TILE = 256
ROW_TILE = 512
CHUNK = 256
SUB = 64
CONV_HALO = 16
ATTN_MAX_KEYS = 2816
ATTN_UNIT_KEYS = 1408
VMEM_LIMIT = 48 * 1024 * 1024


def _cparams(sem):
    return pltpu.CompilerParams(dimension_semantics=sem, vmem_limit_bytes=VMEM_LIMIT)


def _sigmoid(x):
    return 1.0 / (1.0 + jnp.exp(-x))


def _softplus(x):
    return jnp.maximum(x, 0.0) + jnp.log1p(jnp.exp(-jnp.abs(x)))


def _gelu_tanh(x):
    return 0.5 * x * (1.0 + jnp.tanh(0.7978845608028654 * (x + 0.044715 * (x * x * x))))


def _mm(a, b):
    return jnp.dot(a.astype(BF16), b.astype(BF16), preferred_element_type=F32)


def _mm_nt(a, b):
    return lax.dot_general(a.astype(BF16), b.astype(BF16), (((1,), (1,)), ((), ())),
                           preferred_element_type=F32)


def _split2(a):
    hi = a.astype(BF16)
    lo = (a - hi.astype(F32)).astype(BF16)
    return hi, lo


def _split3(a):
    hi = a.astype(BF16)
    r1 = a - hi.astype(F32)
    mid = r1.astype(BF16)
    lo = (r1 - mid.astype(F32)).astype(BF16)
    return hi, mid, lo


def _mm_exact_rhs(a, b_exact):
    hi, lo = _split2(a)
    d = functools.partial(jnp.dot, preferred_element_type=F32)
    return d(hi, b_exact) + d(lo, b_exact)


def _mm_exact_lhs(a_exact, b):
    hi, mid, lo = _split3(b)
    d = functools.partial(jnp.dot, preferred_element_type=F32)
    return d(a_exact, hi) + d(a_exact, mid) + d(a_exact, lo)


def _mm3(a, b):
    ah, al = _split2(a)
    bh, bl = _split2(b)
    d = functools.partial(jnp.dot, preferred_element_type=F32)
    return d(ah, bh) + d(ah, bl) + d(al, bh)


def _mm3_nt(a, b):
    ah, al = _split2(a)
    bh, bl = _split2(b)
    d = functools.partial(lax.dot_general, dimension_numbers=(((1,), (1,)), ((), ())),
                          preferred_element_type=F32)
    return d(ah, bh) + d(ah, bl) + d(al, bh)


def _mm_tn(a, b):
    return lax.dot_general(a.astype(BF16), b.astype(BF16), (((0,), (0,)), ((), ())),
                           preferred_element_type=F32)


def _modulate(x, shift, scale):
    ms = jnp.mean(x * x, axis=-1, keepdims=True)
    return x * lax.rsqrt(ms + EPS) * (1.0 + scale) + shift


def _ada_kernel(c_ref, w_ref, b_ref, o_ref):
    c = c_ref[...]
    s = c * _sigmoid(c)
    o_ref[...] = _mm(s, w_ref[...]) + b_ref[...]


def _ada_call(cvec, ada_w, ada_b):
    n_layers, d, nd = ada_w.shape
    tn = nd // 8
    return pl.pallas_call(
        _ada_kernel,
        out_shape=jax.ShapeDtypeStruct((n_layers, SUBLANE, nd), F32),
        grid=(n_layers, nd // tn),
        in_specs=[
            pl.BlockSpec((SUBLANE, d), lambda l, j: (0, 0)),
            pl.BlockSpec((None, d, tn), lambda l, j: (l, 0, j)),
            pl.BlockSpec((None, 1, tn), lambda l, j: (l, 0, j)),
        ],
        out_specs=pl.BlockSpec((None, SUBLANE, tn), lambda l, j: (l, 0, j)),
        compiler_params=_cparams(("parallel", "parallel")),
        name="ada_mod",
    )(cvec, ada_w, ada_b.reshape(n_layers, 1, nd))


def _ffn_kernel(*refs, row0, d_ff, chunks, split_input, mix, ctx_mix, final):
    refs = list(refs)
    o_ref = refs.pop()
    i = pl.program_id(0)
    if split_input:
        ctx_ref, lat_ref = refs.pop(0), refs.pop(0)
        x = jnp.where(i == 0, ctx_ref[...], lat_ref[...])
    else:
        x = refs.pop(0)[...]
    mod_ref = refs.pop(0)
    if mix:
        ya_ref, yb_ref, yc0_ref, yc1_ref, yd_ref = (refs.pop(0) for _ in range(5))
        yc = jnp.concatenate([yc0_ref[...], yc1_ref[...]], axis=0)
        yd = yd_ref[...]
        if ctx_mix:
            yd = jnp.where(i == 0, refs.pop(0)[...], yd)
        wa_ref, wb_ref, wc_ref, wd_ref = (refs.pop(0) for _ in range(4))
        d = functools.partial(jnp.dot, preferred_element_type=F32)
        y = (d(ya_ref[...], wa_ref[...]) + d(yb_ref[...], wb_ref[...])
             + d(yc, wc_ref[...]) + d(yd, wd_ref[...]))
        x = x + mod_ref[5:6, :] * y
    w13_ref, w2_ref = refs.pop(0), refs.pop(0)
    xm = _modulate(x, mod_ref[row0:row0 + 1, :], mod_ref[row0 + 1:row0 + 2, :]).astype(BF16)
    acc = None
    for lo, hi in chunks:
        g = jnp.dot(xm, w13_ref[:, lo:hi], preferred_element_type=F32)
        u = jnp.dot(xm, w13_ref[:, d_ff + lo:d_ff + hi], preferred_element_type=F32)
        a = ((g * _sigmoid(g)) * u).astype(BF16)
        part = jnp.dot(a, w2_ref[lo:hi, :], preferred_element_type=F32)
        acc = part if acc is None else acc + part
    out = x + 0.5 * mod_ref[row0 + 2:row0 + 3, :] * acc
    if final:
        gain_ref = refs.pop(0)
        out = out * lax.rsqrt(jnp.mean(out * out, axis=-1, keepdims=True) + EPS) * gain_ref[...]
    o_ref[...] = out


def _group_of_tile(i, tiles_per_batch):
    return (i + tiles_per_batch - 1) // tiles_per_batch


def _ffn_call(x, mods, w13, w2, *, layer, row0, tiles_per_batch, ctx_rows, mix=None, final_gain=None):
    split_input = isinstance(x, tuple)
    final = final_gain is not None
    assert not (split_input and final)
    d = w13.shape[1]
    d_ff = w2.shape[1]
    n = ctx_rows + x[1].shape[0] if split_input else x.shape[0]
    skip = ctx_rows // ROW_TILE if final else 0
    assert d_ff % MXU_TILE == 0
    n_tiles = d_ff // MXU_TILE
    split = ((n_tiles + 1) // 2) * MXU_TILE
    chunks = ((0, split), (split, d_ff))
    resident = pl.Buffered(1)
    lat_tile = lambda i: jnp.maximum(i + skip - ctx_rows // ROW_TILE, 0)
    row_spec = lambda w: pl.BlockSpec((ROW_TILE, w), lambda i: (i + skip, 0))
    weight_spec = lambda w: pl.BlockSpec((None,) + w.shape[1:], lambda i: (layer, 0, 0), pipeline_mode=resident)

    in_specs, args = [], []
    if split_input:
        in_specs += [pl.BlockSpec((ROW_TILE, d), lambda i: (0, 0)),
                     pl.BlockSpec((ROW_TILE, d), lambda i: (lat_tile(i), 0))]
        args += list(x)
    else:
        in_specs.append(row_spec(d))
        args.append(x)
    in_specs.append(pl.BlockSpec((None, N_MOD, d),
                                 lambda i: (_group_of_tile(i + skip, tiles_per_batch), 0, 0)))
    args.append(mods)
    ctx_mix = False
    if mix is not None:
        ya, yb, yc, yd_lat, yd_ctx, wo_parts = mix
        assert ROW_TILE == 2 * CHUNK
        n_b = ctx_rows // CHUNK
        cpb = tiles_per_batch * (ROW_TILE // CHUNK)
        yc_half = lambda half: pl.BlockSpec(
            (CHUNK, yc.shape[1]), lambda i: (_pb_chunk(2 * (i + skip) + half, n_b, cpb), 0))
        in_specs += [row_spec(ya.shape[1]), row_spec(yb.shape[1]), yc_half(0), yc_half(1),
                     pl.BlockSpec((ROW_TILE, yd_lat.shape[1]), lambda i: (lat_tile(i), 0))]
        args += [ya, yb, yc, yc, yd_lat]
        ctx_mix = yd_ctx is not None
        if ctx_mix:
            in_specs.append(pl.BlockSpec((ROW_TILE, yd_ctx.shape[1]), lambda i: (0, 0)))
            args.append(yd_ctx)
        in_specs += [weight_spec(w) for w in wo_parts]
        args += list(wo_parts)
    in_specs += [weight_spec(w13), weight_spec(w2)]
    args += [w13, w2]
    if final:
        in_specs.append(pl.BlockSpec((1, d), lambda i: (0, 0)))
        args.append(final_gain.reshape(1, d))
    kern = functools.partial(_ffn_kernel, row0=row0, d_ff=d_ff, chunks=chunks, split_input=split_input,
                             mix=mix is not None, ctx_mix=ctx_mix, final=final)
    out_rows = n - skip * ROW_TILE
    return pl.pallas_call(
        kern,
        out_shape=jax.ShapeDtypeStruct((out_rows, d), F32),
        grid=(out_rows // ROW_TILE,),
        in_specs=in_specs,
        out_specs=pl.BlockSpec((ROW_TILE, d), lambda i: (i, 0)),
        compiler_params=_cparams(("parallel",)),
        name="ffn_mix" if mix is not None else "ffn",
    )(*args)


def _win_kernel(x_ref, mod_ref, wa_ref, wb_ref, wc_ref, wd_ref, oa_ref, ob_ref, oc_ref, od_ref):
    xm = _modulate(x_ref[...], mod_ref[3:4, :], mod_ref[4:5, :]).astype(BF16)
    oa_ref[...] = jnp.dot(xm, wa_ref[...], preferred_element_type=F32)
    ob_ref[...] = jnp.dot(xm, wb_ref[...], preferred_element_type=F32)
    oc_ref[...] = jnp.dot(xm, wc_ref[...], preferred_element_type=F32)
    od_ref[...] = jnp.dot(xm, wd_ref[...], preferred_element_type=F32)


def _win_call(x, mods, wa, wb, wc, wd, *, layer, n_batch, cpb):
    n, d = x.shape
    widths = (wa.shape[2], wb.shape[2], wc.shape[2], wd.shape[2])
    group = lambda c: jnp.where(c < n_batch, 0, 1 + jnp.maximum(c - n_batch, 0) // cpb)
    flat = lambda c: (c, 0)
    per_batch = lambda c: (_pb_chunk(c, n_batch, cpb), 0)
    out_maps = (flat, flat, per_batch, flat)
    return pl.pallas_call(
        _win_kernel,
        out_shape=tuple(jax.ShapeDtypeStruct((n, w), F32) for w in widths),
        grid=(n // CHUNK,),
        in_specs=[
            pl.BlockSpec((CHUNK, d), flat),
            pl.BlockSpec((None, N_MOD, d), lambda c: (group(c), 0, 0)),
        ] + [pl.BlockSpec((None, d, w), lambda c: (layer, 0, 0)) for w in widths],
        out_specs=tuple(pl.BlockSpec((CHUNK, w), m) for w, m in zip(widths, out_maps)),
        compiler_params=_cparams(("parallel",)),
        name="w_in",
    )(x, mods, wa, wb, wc, wd)


def _seq_flags(c, n_batch, cpb):
    j = lax.rem(jnp.maximum(c - n_batch, 0), cpb)
    is_ctx = c < n_batch
    first = jnp.logical_or(is_ctx, j == 0)
    last = jnp.logical_or(is_ctx, j == cpb - 1)
    return first, last


def _scan_chunk(b, i, n_batch, cpb, reverse):
    if reverse:
        lat = n_batch + b * cpb + (cpb - i)
    else:
        lat = n_batch + b * cpb + (i - 1)
    return jnp.where(i == 0, b, lat)


def _pb_chunk(c, n_batch, cpb):
    lat = jnp.maximum(c - n_batch, 0)
    b = lat // cpb
    return jnp.where(c < n_batch, c * (cpb + 1), b * (cpb + 1) + 1 + (lat - b * cpb))


def _pb_scan_chunk(i, cpb, reverse):
    if reverse:
        return jnp.where(i == 0, 0, cpb + 1 - i)
    return i


def _conv_kernel(cur_ref, prev_ref, next_ref, w_ref, b_ref, g_ref, beta_ref, o_ref, zbuf, zrot,
                 *, n_batch, cpb):
    c = pl.program_id(0)
    first, last = _seq_flags(c, n_batch, cpb)

    def glu(u):
        return u[:, :GROUP_W] * _sigmoid(u[:, GROUP_W:])

    zp = glu(prev_ref[...])
    zn = glu(next_ref[...])
    zbuf[0:CONV_HALO, :] = jnp.where(first, 0.0, zp)
    zbuf[CONV_HALO:CONV_HALO + CHUNK, :] = glu(cur_ref[...])
    zbuf[CONV_HALO + CHUNK:2 * CONV_HALO + CHUNK, :] = jnp.where(last, 0.0, zn)

    pad = CONV_K // 2
    span = CHUNK + 2 * CONV_HALO - SUBLANE
    acc = jnp.zeros((CHUNK, GROUP_W), F32) + b_ref[...]
    for phase in range(SUBLANE):
        if phase:
            zrot[phase - 1] = zbuf[phase:phase + span, :]
        for j in range(CONV_K):
            start = CONV_HALO - pad + j
            if start % SUBLANE == phase:
                lo = start - phase
                win = zrot[phase - 1, lo:lo + CHUNK, :] if phase else zbuf[lo:lo + CHUNK, :]
                acc = acc + w_ref[j:j + 1, :] * win
    mu = jnp.mean(acc, axis=-1, keepdims=True)
    xc = acc - mu
    var = jnp.mean(xc * xc, axis=-1, keepdims=True)
    y = xc * lax.rsqrt(var + LN_EPS) * g_ref[...] + beta_ref[...]
    o_ref[...] = (y * _sigmoid(y)).astype(o_ref.dtype)


def _conv_call(u_a, dw_w, dw_b, ln_g, ln_b, *, n_batch, cpb):
    n = u_a.shape[0]
    n_chunks = n // CHUNK
    hpc = CHUNK // CONV_HALO
    n_halo = n // CONV_HALO
    w_pad = jnp.zeros((32, GROUP_W), F32).at[:CONV_K].set(dw_w)
    row = lambda v: v.reshape(1, GROUP_W)
    const = lambda c: (0, 0)
    kern = functools.partial(_conv_kernel, n_batch=n_batch, cpb=cpb)
    return pl.pallas_call(
        kern,
        out_shape=jax.ShapeDtypeStruct((n, GROUP_W), BF16),
        grid=(n_chunks,),
        in_specs=[
            pl.BlockSpec((CHUNK, 2 * GROUP_W), lambda c: (c, 0)),
            pl.BlockSpec((CONV_HALO, 2 * GROUP_W), lambda c: (jnp.maximum(c * hpc - 1, 0), 0)),
            pl.BlockSpec((CONV_HALO, 2 * GROUP_W),
                         lambda c: (jnp.minimum((c + 1) * hpc, n_halo - 1), 0)),
            pl.BlockSpec((32, GROUP_W), const),
            pl.BlockSpec((1, GROUP_W), const),
            pl.BlockSpec((1, GROUP_W), const),
            pl.BlockSpec((1, GROUP_W), const),
        ],
        out_specs=pl.BlockSpec((CHUNK, GROUP_W), lambda c: (c, 0)),
        scratch_shapes=[pltpu.VMEM((CHUNK + 2 * CONV_HALO, GROUP_W), F32),
                        pltpu.VMEM((SUBLANE - 1, CHUNK + 2 * CONV_HALO - SUBLANE, GROUP_W), F32)],
        compiler_params=_cparams(("parallel",)),
        name="conformer_conv",
    )(u_a, u_a, u_a, w_pad, row(dw_b), row(ln_g), row(ln_b))


def _lru_kernel(*refs, n_batch, cpb, reverse):
    if reverse:
        cur_ref, prev_ref, next_ref, cw_ref, cb_ref, wg_ref, bg_ref, lam_ref, o_ref, xbuf, carry = refs
        hrev_ref = None
    else:
        (cur_ref, prev_ref, next_ref, cw_ref, cb_ref, wg_ref, bg_ref, lam_ref, hrev_ref,
         o_ref, xbuf, carry) = refs
    b = pl.program_id(0)
    i = pl.program_id(1)
    c = _scan_chunk(b, i, n_batch, cpb, reverse)
    first, last = _seq_flags(c, n_batch, cpb)

    @pl.when(i == 0)
    def _():
        carry[...] = jnp.zeros_like(carry)

    u = cur_ref[...]
    gb = u[:, GROUP_W:]
    xbuf[0:SUBLANE, :] = jnp.where(first, 0.0, prev_ref[:, :GROUP_W])
    xbuf[SUBLANE:SUBLANE + CHUNK, :] = u[:, :GROUP_W]
    xbuf[SUBLANE + CHUNK:2 * SUBLANE + CHUNK, :] = jnp.where(last, 0.0, next_ref[:, :GROUP_W])
    pad_l = LRU_CONV_K // 2
    xv = jnp.zeros((CHUNK, GROUP_W), F32) + cb_ref[...]
    for j in range(LRU_CONV_K):
        start = SUBLANE - pad_l + j
        xv = xv + cw_ref[j:j + 1, :] * xbuf[start:start + CHUNK, :]

    z = _mm(xv, wg_ref[...]) + bg_ref[...]
    r = _sigmoid(z[:, :GROUP_W])
    ig = _sigmoid(z[:, GROUP_W:])
    log_a = (-LRU_C) * r * _softplus(-lam_ref[...])
    a = jnp.exp(log_a)
    t = jnp.tanh(log_a)
    bb = jnp.sqrt(-2.0 * t / (1.0 - t)) * (ig * xv)

    row = lax.broadcasted_iota(jnp.int32, (CHUNK, GROUP_W), 0)
    s = 1
    while s < CHUNK:
        if reverse:
            a_sh = pltpu.roll(a, CHUNK - s, axis=0)
            b_sh = pltpu.roll(bb, CHUNK - s, axis=0)
            valid = row < CHUNK - s
        else:
            a_sh = pltpu.roll(a, s, axis=0)
            b_sh = pltpu.roll(bb, s, axis=0)
            valid = row >= s
        bb = jnp.where(valid, a * b_sh + bb, bb)
        a = jnp.where(valid, a * a_sh, a)
        s *= 2
    h = bb + a * carry[0:1, :]
    edge = h[0:1, :] if reverse else h[CHUNK - 1:CHUNK, :]
    carry[...] = jnp.broadcast_to(edge, carry.shape)

    if reverse:
        o_ref[...] = h
    else:
        o_ref[...] = ((h + hrev_ref[...]) * _gelu_tanh(gb)).astype(o_ref.dtype)


def _lru_call(u_b, conv_w, conv_b, w_gate, b_gate, lam, h_rev, *, n_batch, cpb, reverse):
    n = u_b.shape[0]
    bpc = CHUNK // SUBLANE
    n_blk = n // SUBLANE
    chunk = lambda b, i: _scan_chunk(b, i, n_batch, cpb, reverse)
    const = lambda b, i: (0, 0)
    cw_pad = jnp.zeros((SUBLANE, GROUP_W), F32).at[:LRU_CONV_K].set(conv_w)
    in_specs = [
        pl.BlockSpec((CHUNK, 2 * GROUP_W), lambda b, i: (chunk(b, i), 0)),
        pl.BlockSpec((SUBLANE, 2 * GROUP_W), lambda b, i: (jnp.maximum(chunk(b, i) * bpc - 1, 0), 0)),
        pl.BlockSpec((SUBLANE, 2 * GROUP_W),
                     lambda b, i: (jnp.minimum((chunk(b, i) + 1) * bpc, n_blk - 1), 0)),
        pl.BlockSpec((SUBLANE, GROUP_W), const),
        pl.BlockSpec((1, GROUP_W), const),
        pl.BlockSpec((GROUP_W, 2 * GROUP_W), const),
        pl.BlockSpec((1, 2 * GROUP_W), const),
        pl.BlockSpec((1, GROUP_W), const),
    ]
    args = [u_b, u_b, u_b, cw_pad, conv_b.reshape(1, GROUP_W), w_gate, b_gate.reshape(1, 2 * GROUP_W),
            lam.reshape(1, GROUP_W)]
    if not reverse:
        in_specs.append(pl.BlockSpec((CHUNK, GROUP_W), lambda b, i: (chunk(b, i), 0)))
        args.append(h_rev)
    kern = functools.partial(_lru_kernel, n_batch=n_batch, cpb=cpb, reverse=reverse)
    return pl.pallas_call(
        kern,
        out_shape=jax.ShapeDtypeStruct((n, GROUP_W), F32 if reverse else BF16),
        grid=(n_batch, cpb + 1),
        in_specs=in_specs,
        out_specs=pl.BlockSpec((CHUNK, GROUP_W), lambda b, i: (chunk(b, i), 0)),
        scratch_shapes=[pltpu.VMEM((CHUNK + 2 * SUBLANE, GROUP_W), F32),
                        pltpu.VMEM((SUBLANE, GROUP_W), F32)],
        compiler_params=_cparams(("arbitrary", "arbitrary")),
        name="rglru_rev" if reverse else "rglru_fwd",
    )(*args)


def _rwkv_kernel(*refs, n_batch, cpb, reverse):
    if reverse:
        (cur_ref, prev_ref, next_ref, mup_ref, mun_ref, w0_ref, wup_ref, a0_ref, aup_ref, gup_ref,
         kk_ref, ka_ref, rk_ref, gng_ref, gnb_ref, hones_ref,
         y_ref, bonus_ref, ubuf, st_scr) = refs
        yrev_ref = brev_ref = None
    else:
        (cur_ref, prev_ref, next_ref, mup_ref, mun_ref, w0_ref, wup_ref, a0_ref, aup_ref, gup_ref,
         kk_ref, ka_ref, rk_ref, gng_ref, gnb_ref, hones_ref, yrev_ref, brev_ref,
         y_ref, ubuf, st_scr) = refs
    i = pl.program_id(0)
    j = _pb_scan_chunk(i, cpb, reverse)
    first = jnp.logical_or(i == 0, j == 1)
    last = jnp.logical_or(i == 0, j == cpb)
    g4 = RWKV_HEADS * RWKV_HEAD
    assert g4 == GROUP_W and RWKV_HEADS * SUB == GROUP_W
    rows_all = n_batch * CHUNK
    wc = cur_ref.shape[-1]

    @pl.when(i == 0)
    def _():
        st_scr[...] = jnp.zeros_like(st_scr)

    ubuf[:, 0:SUBLANE, :] = jnp.where(first, 0.0, prev_ref[...])
    ubuf[:, SUBLANE:SUBLANE + CHUNK, :] = cur_ref[...]
    ubuf[:, SUBLANE + CHUNK:2 * SUBLANE + CHUNK, :] = jnp.where(last, 0.0, next_ref[...])
    u = cur_ref[...].reshape(rows_all, wc)
    up = ubuf[:, SUBLANE - 1:SUBLANE - 1 + CHUNK, :].reshape(rows_all, wc)
    un = ubuf[:, SUBLANE + 1:SUBLANE + 1 + CHUNK, :].reshape(rows_all, wc)
    vs = u + mup_ref[...] * (up - u) + mun_ref[...] * (un - u)

    hones = hones_ref[...]
    r = vs[:, 0:GROUP_W]
    k = vs[:, GROUP_W:2 * GROUP_W]
    val = vs[:, 2 * GROUP_W:3 * GROUP_W]
    lora_in = vs[:, 3 * GROUP_W:3 * GROUP_W + LANE]
    gate_in = vs[:, 3 * GROUP_W + LANE:]
    w_lin = _mm(jnp.tanh(lora_in), wup_ref[...])
    a_lin = _mm(lora_in, aup_ref[...])
    kq = k * kk_ref[...]
    ss = _mm_exact_rhs(kq * kq, hones)
    kk = kq * lax.rsqrt(jnp.maximum(ss, 1e-24))
    lw = -jnp.exp(-_softplus(-(w0_ref[...] + w_lin)) - 0.5)
    a = _sigmoid(a0_ref[...] + a_lin)
    kd = k * (1.0 + (a - 1.0) * ka_ref[...])
    bonus = _mm_exact_rhs(r * kd * rk_ref[...], hones) * val

    bvec = kk * a

    ri = lax.broadcasted_iota(jnp.int32, (g4, g4), 0)
    ci = lax.broadcasted_iota(jnp.int32, (g4, g4), 1)
    same = (ri // SUB) == (ci // SUB)
    rt = lax.rem(ri, SUB)
    ct = lax.rem(ci, SUB)
    if reverse:
        strict = jnp.logical_and(same, ct > rt)
        incl = jnp.logical_and(same, ct >= rt)
    else:
        strict = jnp.logical_and(same, ct < rt)
        incl = jnp.logical_and(same, ct <= rt)
    eye = ri == ci
    ti = lax.broadcasted_iota(jnp.int32, (SUB, SUB), 0)
    tj = lax.broadcasted_iota(jnp.int32, (SUB, SUB), 1)
    tri = jnp.where((tj >= ti) if reverse else (tj <= ti), 1.0, 0.0).astype(BF16)
    lane_head = lax.broadcasted_iota(jnp.int32, (1, g4), 1) // RWKV_HEAD

    def stack(x):
        return jnp.concatenate([jnp.where(lane_head == h, x, 0.0) for h in range(RWKV_HEADS)], axis=0)

    def unstack(x):
        out = x[0:SUB, :]
        for h in range(1, RWKV_HEADS):
            out = out + x[h * SUB:(h + 1) * SUB, :]
        return out

    per_batch = CHUNK // SUB
    n_sub = n_batch * per_batch
    subs = range(n_sub)
    rows = [slice(sc * SUB, (sc + 1) * SUB) for sc in subs]
    each = lambda f, *lists: [f(*xs) for xs in zip(*lists)]
    lw_s = [lw[rw] for rw in rows]
    kk_s = [kk[rw] for rw in rows]
    b_s = [bvec[rw] for rw in rows]
    kd_s = [kd[rw] for rw in rows]
    r_s = [r[rw] for rw in rows]
    v_s = [val[rw] for rw in rows]

    cum = each(lambda x: _mm_exact_lhs(tri, x), lw_s)
    edge = each(lambda x: x[0:1, :] if reverse else x[SUB - 1:SUB, :], cum)
    a_st = each(lambda cu, l, x: stack(jnp.exp(cu - l) * x), cum, lw_s, kk_s)
    b_st = each(lambda cu, x: stack(x * jnp.exp(-cu)), cum, b_s)
    k_st = each(lambda cu, x: stack(x * jnp.exp(-cu)), cum, kd_s)
    r_st = each(lambda cu, x: stack(x * jnp.exp(cu)), cum, r_s)
    v_bf = each(lambda x: stack(x).astype(BF16), v_s)
    e_out = each(lambda ed, cu: jnp.exp(ed - cu), edge, cum)
    beta_w = each(lambda x, e: x * e, b_s, e_out)
    kappa_w = each(lambda x, e: x * e, kd_s, e_out)
    w_all = each(jnp.exp, edge)

    prod = each(lambda a_, r_, b_, k_: _mm_nt(jnp.concatenate([a_, r_], axis=0),
                                              jnp.concatenate([b_, k_], axis=0)),
                a_st, r_st, b_st, k_st)
    l_ak = each(lambda p_: jnp.where(strict, p_[0:g4, g4:], 0.0), prod)
    m_rb = each(lambda p_: jnp.where(incl, p_[g4:, 0:g4], 0.0).astype(BF16), prod)
    m_rk = each(lambda p_: jnp.where(incl, p_[g4:, g4:], 0.0), prod)

    l_ab = each(lambda p_: jnp.where(strict, p_[0:g4, 0:g4], 0.0), prod)
    q = each(lambda l_: -l_, l_ab)
    tm = q
    for _ in range(5):
        q = each(lambda q_: _mm(q_, q_), q)
        tm = each(lambda t_, q_: t_ + q_ + _mm(t_, q_), tm, q)

    def refine(t_, l_):
        res = -(t_ + l_) - _mm3(l_, t_)
        return t_ + res + _mm(t_, res)

    tm = each(refine, tm, l_ab)

    lakv = each(_mm, l_ak, v_bf)
    rhs = each(lambda a_, x: jnp.concatenate([a_, x], axis=1), a_st, lakv)
    sol = each(lambda t_, x: x + _mm(t_, x), tm, rhs)
    corr = each(_mm, m_rb, sol)
    ra_st = each(lambda r_, c_: r_ - c_[:, 0:g4], r_st, corr)
    y0_st = each(lambda m_, v_, c_: _mm(m_, v_) - c_[:, g4:], m_rk, v_bf, corr)
    ta = each(lambda s_: unstack(s_[:, 0:g4]), sol)
    z0 = each(lambda s_: unstack(s_[:, g4:]), sol)
    gt = each(lambda w_, bw, ta_: jnp.where(eye, w_, 0.0) - jnp.where(same, _mm_tn(bw, ta_), 0.0),
              w_all, beta_w, ta)
    ht = each(lambda kw, v_, bw, z_: jnp.where(same, _mm_tn(kw, v_) - _mm_tn(bw, z_), 0.0),
              kappa_w, v_s, beta_w, z0)

    order = list(range(per_batch - 1, -1, -1)) if reverse else list(range(per_batch))
    st = [st_scr[bi] for bi in range(n_batch)]
    y_parts = {}
    for t in order:
        for bi in range(n_batch):
            sc = bi * per_batch + t
            y_parts[sc] = unstack(_mm(ra_st[sc], st[bi]) + y0_st[sc])
            st[bi] = _mm3(gt[sc], st[bi]) + ht[sc]
    for bi in range(n_batch):
        st_scr[bi] = st[bi]
    y_all = jnp.concatenate([y_parts[sc] for sc in subs], axis=0)

    if reverse:
        y_ref[...] = y_all.reshape(n_batch, CHUNK, GROUP_W)
        bonus_ref[...] = bonus.reshape(n_batch, CHUNK, GROUP_W)
    else:
        yy = y_all + yrev_ref[...].reshape(rows_all, GROUP_W)
        inv_n = 1.0 / RWKV_HEAD
        mu = _mm_exact_rhs(yy, hones) * inv_n
        yc = yy - mu
        var = _mm_exact_rhs(yc * yc, hones) * inv_n
        o = (yc * lax.rsqrt(var + GN_EPS) * gng_ref[...] + gnb_ref[...] + bonus
             + brev_ref[...].reshape(rows_all, GROUP_W))
        gate = _mm(_sigmoid(gate_in), gup_ref[...])
        y_ref[...] = (o * gate).astype(y_ref.dtype).reshape(n_batch, CHUNK, GROUP_W)


def _rwkv_call(u_c, p, y_rev, bonus_rev, *, n_batch, cpb, reverse):
    n, wc = u_c.shape
    bpc = CHUNK // SUBLANE
    cps = cpb + 1
    assert n == n_batch * cps * CHUNK
    chunk = lambda i: _pb_scan_chunk(i, cpb, reverse)
    const = lambda i: (0, 0)
    u4 = u_c.reshape(n_batch, cps, CHUNK, wc)
    u8 = u_c.reshape(n_batch, cps * bpc, SUBLANE, wc)
    d = 1 if reverse else 0
    row = lambda v: v.reshape(1, -1)
    zeros64 = jnp.zeros((RWKV_HEAD, GROUP_W), F32)
    wup = jnp.concatenate([p['w_up'][d], zeros64], axis=0).astype(BF16)
    aup = jnp.concatenate([zeros64, p['a_up'][d]], axis=0).astype(BF16)
    hid = jnp.arange(GROUP_W) // RWKV_HEAD
    hones = (hid[:, None] == hid[None, :]).astype(BF16)
    small = [row(p['mu_prev']), row(p['mu_next']), row(p['w0'][d]), wup, row(p['a0'][d]), aup,
             p['g_up'].astype(BF16), row(p['k_k']), row(p['k_a']), row(p['r_k']), row(p['gn_g']),
             row(p['gn_b']), hones]
    in_specs = [
        pl.BlockSpec((n_batch, None, CHUNK, wc), lambda i: (0, chunk(i), 0, 0)),
        pl.BlockSpec((n_batch, None, SUBLANE, wc), lambda i: (0, jnp.maximum(chunk(i) * bpc - 1, 0), 0, 0)),
        pl.BlockSpec((n_batch, None, SUBLANE, wc),
                     lambda i: (0, jnp.minimum((chunk(i) + 1) * bpc, cps * bpc - 1), 0, 0)),
    ] + [pl.BlockSpec(s.shape, const) for s in small]
    args = [u4, u8, u8] + small
    seq_spec = pl.BlockSpec((n_batch, None, CHUNK, GROUP_W), lambda i: (0, chunk(i), 0, 0))
    seq_shape = (n_batch, cps, CHUNK, GROUP_W)
    scratch = [pltpu.VMEM((n_batch, CHUNK + 2 * SUBLANE, wc), F32),
               pltpu.VMEM((n_batch, GROUP_W, GROUP_W), F32)]
    if reverse:
        out_shape = (jax.ShapeDtypeStruct(seq_shape, F32), jax.ShapeDtypeStruct(seq_shape, F32))
        out_specs = (seq_spec, seq_spec)
    else:
        in_specs += [seq_spec, seq_spec]
        args += [y_rev, bonus_rev]
        out_shape = jax.ShapeDtypeStruct(seq_shape, BF16)
        out_specs = seq_spec
    kern = functools.partial(_rwkv_kernel, n_batch=n_batch, cpb=cpb, reverse=reverse)
    out = pl.pallas_call(
        kern,
        out_shape=out_shape,
        grid=(cps,),
        in_specs=in_specs,
        out_specs=out_specs,
        scratch_shapes=scratch,
        compiler_params=_cparams(("arbitrary",)),
        name="rwkv7_rev" if reverse else "rwkv7_fwd",
    )(*args)
    return out if reverse else out.reshape(n, GROUP_W)


def _mla_proj_kernel(u_ref, cos_ref, sin_ref, qn_ref, kvn_ref, wqm_ref, wqr_ref, wk_ref, wv_ref,
                     q_ref, k_ref, v_ref):
    u = u_ref[...]
    cq = u[:, :Q_LORA]
    ckv = u[:, Q_LORA:Q_LORA + KV_LORA]
    blk_r = u[:, Q_LORA + KV_LORA:Q_LORA + KV_LORA + LANE]
    blk_rr = u[:, Q_LORA + KV_LORA + LANE:]
    cqn = (cq * lax.rsqrt(jnp.mean(cq * cq, axis=-1, keepdims=True) + EPS) * qn_ref[...]).astype(BF16)
    ckvn = (ckv * lax.rsqrt(jnp.mean(ckv * ckv, axis=-1, keepdims=True) + EPS) * kvn_ref[...]).astype(BF16)
    cos_t = cos_ref[...]
    sin_t = sin_ref[...]
    cos4 = jnp.concatenate([cos_t] * MLA_HEADS, axis=1)
    sin4 = jnp.concatenate([sin_t] * MLA_HEADS, axis=1)
    qm = jnp.dot(cqn, wqm_ref[...], preferred_element_type=F32)
    qr = jnp.dot(cqn, wqr_ref[...], preferred_element_type=F32)
    q_ref[...] = ((qm * cos4 + qr * sin4) * (SM_SCALE * LOG2_E)).astype(q_ref.dtype)
    kr = blk_r * cos_t + blk_rr * sin_t
    km = jnp.dot(ckvn, wk_ref[...], preferred_element_type=F32)
    k_ref[...] = (km + jnp.concatenate([kr] * MLA_HEADS, axis=1)).astype(k_ref.dtype)
    lane = lax.broadcasted_iota(jnp.int32, (1, MLA_HEADS * LANE), 1)
    ones_col = jnp.where(lax.rem(lane, LANE) == V_HEAD, 1.0, 0.0)
    v_ref[...] = (jnp.dot(ckvn, wv_ref[...], preferred_element_type=F32) + ones_col).astype(v_ref.dtype)


def _mla_proj_call(u_d, cos_tab, sin_tab, q_norm, kv_norm, wqm, wqr, wk, wv, *, n_batch, cpb):
    n, wd = u_d.shape
    hw = MLA_HEADS * LANE
    kv_cpb = cpb + 1

    def kv_chunk(c):
        lat = c - n_batch
        bb = lat // cpb
        return jnp.where(c < n_batch, c * kv_cpb + cpb, bb * kv_cpb + (lat - bb * cpb))

    def tab_chunk(c):
        return jnp.where(c < n_batch, 0, 1 + lax.rem(jnp.maximum(c - n_batch, 0), cpb))

    const = lambda c: (0, 0)
    return pl.pallas_call(
        _mla_proj_kernel,
        out_shape=(jax.ShapeDtypeStruct((n, hw), BF16), jax.ShapeDtypeStruct((n, hw), BF16),
                   jax.ShapeDtypeStruct((n, hw), BF16)),
        grid=(n // CHUNK,),
        in_specs=[
            pl.BlockSpec((CHUNK, wd), lambda c: (c, 0)),
            pl.BlockSpec((CHUNK, LANE), lambda c: (tab_chunk(c), 0)),
            pl.BlockSpec((CHUNK, LANE), lambda c: (tab_chunk(c), 0)),
            pl.BlockSpec((1, Q_LORA), const),
            pl.BlockSpec((1, KV_LORA), const),
            pl.BlockSpec((Q_LORA, hw), const),
            pl.BlockSpec((Q_LORA, hw), const),
            pl.BlockSpec((KV_LORA, hw), const),
            pl.BlockSpec((KV_LORA, hw), const),
        ],
        out_specs=(pl.BlockSpec((CHUNK, hw), lambda c: (c, 0)),
                   pl.BlockSpec((CHUNK, hw), lambda c: (kv_chunk(c), 0)),
                   pl.BlockSpec((CHUNK, hw), lambda c: (kv_chunk(c), 0))),
        compiler_params=_cparams(("parallel",)),
        name="mla_proj",
    )(u_d, cos_tab, sin_tab, q_norm.reshape(1, Q_LORA), kv_norm.reshape(1, KV_LORA), wqm, wqr, wk, wv)


def _attn_kernel(q_ref, k_ref, v_ref, o_ref, m_scr, acc_scr, *, n_kv, unit_keys):
    kj = pl.program_id(2)

    @pl.when(kj == 0)
    def _():
        m_scr[...] = jnp.full_like(m_scr, -jnp.inf)
        acc_scr[...] = jnp.zeros_like(acc_scr)

    tk = k_ref.shape[0]
    bounds = list(range(0, tk, unit_keys)) + [tk]
    units = [(h, lo, hi) for h in range(MLA_HEADS) for lo, hi in zip(bounds[:-1], bounds[1:])]

    def scores(u):
        h, lo, hi = u
        sl = slice(h * LANE, (h + 1) * LANE)
        return lax.dot_general(q_ref[:, sl], k_ref[lo:hi, sl], (((1,), (1,)), ((), ())),
                               preferred_element_type=F32)

    def softmax_step(u, s):
        h = u[0]
        m_prev = m_scr[h]
        m_new = jnp.maximum(m_prev, jnp.max(s, axis=-1, keepdims=True))
        m_scr[h] = m_new
        return jnp.exp2(s - m_new).astype(BF16), jnp.exp2(m_prev - m_new)

    def weighted_values(u, p, alpha):
        h, lo, hi = u
        sl = slice(h * LANE, (h + 1) * LANE)
        acc_scr[h] = alpha * acc_scr[h] + jnp.dot(p, v_ref[lo:hi, sl], preferred_element_type=F32)

    s_next = scores(units[0])
    pending = None
    for idx, u in enumerate(units):
        s_cur = s_next
        if idx + 1 < len(units):
            s_next = scores(units[idx + 1])
        p, alpha = softmax_step(u, s_cur)
        if pending is not None:
            weighted_values(*pending)
        pending = (u, p, alpha)
    weighted_values(*pending)

    @pl.when(kj == n_kv - 1)
    def _():
        for h in range(MLA_HEADS):
            sl = slice(h * LANE, (h + 1) * LANE)
            acc = acc_scr[h]
            o_ref[:, sl] = (acc / acc[:, V_HEAD:V_HEAD + 1]).astype(o_ref.dtype)


def _attn_call(q, k, v, *, out_rows, n_batch, tq, tk, n_q, n_kv, q_block, kv_block, o_block, name):
    hw = q.shape[1]
    unit = min(ATTN_UNIT_KEYS, tk)
    kern = functools.partial(_attn_kernel, n_kv=n_kv, unit_keys=unit)
    return pl.pallas_call(
        kern,
        out_shape=jax.ShapeDtypeStruct((out_rows, hw), BF16),
        grid=(n_batch, n_q, n_kv),
        in_specs=[
            pl.BlockSpec((tq, hw), lambda b, qi, kj: (q_block(b, qi), 0)),
            pl.BlockSpec((tk, hw), lambda b, qi, kj: (kv_block(b, kj), 0)),
            pl.BlockSpec((tk, hw), lambda b, qi, kj: (kv_block(b, kj), 0)),
        ],
        out_specs=pl.BlockSpec((tq, hw), lambda b, qi, kj: (o_block(b, qi), 0)),
        scratch_shapes=[pltpu.VMEM((MLA_HEADS, tq, 1), F32), pltpu.VMEM((MLA_HEADS, tq, LANE), F32)],
        compiler_params=_cparams(("parallel", "parallel", "arbitrary")),
        name=name,
    )(q, k, v)


def _kv_tile(n_keys):
    best = LANE
    t = LANE
    while t <= ATTN_MAX_KEYS:
        if n_keys % t == 0:
            best = t
        t += LANE
    return best


def _rot_cols(w):
    q = QK_ROPE // 4
    return jnp.concatenate([-w[..., q:2 * q], w[..., 0:q], -w[..., 3 * q:4 * q], w[..., 2 * q:3 * q]],
                           axis=-1)


def _block_diag(w):
    nb, n = w.shape[-3], w.shape[-2]
    eye = jnp.eye(nb, dtype=w.dtype)
    out = eye[:, None, :, None] * w[..., :, :, None, :]
    return out.reshape(w.shape[:-3] + (nb * n, nb * n))


def _rope_tables(t_len, ctx_len):
    t = jnp.arange(t_len, dtype=jnp.int32)
    rows = (t // GRID_W).astype(F32)
    cols = (t % GRID_W).astype(F32)
    n_freq = QK_ROPE // 4
    inv_freq = ROPE_BASE ** (-jnp.arange(n_freq, dtype=F32) / n_freq)
    ang = jnp.stack([rows[:, None] * inv_freq, cols[:, None] * inv_freq], axis=1)
    ang = jnp.concatenate([ang, ang], axis=-1).reshape(t_len, QK_ROPE)
    cos = jnp.concatenate([jnp.ones((ctx_len, QK_ROPE), F32), jnp.cos(ang)], axis=0)
    sin = jnp.concatenate([jnp.zeros((ctx_len, QK_ROPE), F32), jnp.sin(ang)], axis=0)
    n = t_len + ctx_len
    pad = LANE - QK_NOPE - QK_ROPE
    cos_tab = jnp.concatenate([jnp.ones((n, QK_NOPE), F32), cos, jnp.zeros((n, pad), F32)], axis=1)
    sin_tab = jnp.concatenate([jnp.zeros((n, QK_NOPE), F32), sin, jnp.zeros((n, pad), F32)], axis=1)
    return cos_tab, sin_tab


def kernel(x, c, ctx, c_ctx, ada_w, ada_b, ffn1_w13, ffn1_w2, ffn2_w13, ffn2_w2, w_in, w_out,
           cv_dw_w, cv_dw_b, cv_ln_g, cv_ln_b,
           lru_conv_w, lru_conv_b, lru_wa, lru_ba, lru_wx, lru_bx, lru_lambda,
           rwkv_mu_prev, rwkv_mu_next, rwkv_w0, rwkv_w_up, rwkv_a0, rwkv_a_up, rwkv_g_up,
           rwkv_k_k, rwkv_k_a, rwkv_r_k, rwkv_gn_g, rwkv_gn_b,
           mla_q_norm, mla_w_uq, mla_kv_norm, mla_w_ukv, final_norm):
    n_batch, t_len, d = x.shape
    ctx_len = ctx.shape[1]
    depth = ada_w.shape[0]
    assert ctx_len == CHUNK and n_batch * ctx_len == ROW_TILE and t_len % ROW_TILE == 0
    assert c.shape[0] + 1 <= SUBLANE
    ctx_rows = n_batch * ctx_len
    cpb = t_len // CHUNK
    tpb = t_len // ROW_TILE
    g = GROUP_W

    cvec = jnp.concatenate([c_ctx[None, :], c, jnp.zeros((SUBLANE - 1 - n_batch, d), F32)], axis=0)
    mods_all = _ada_call(cvec, ada_w, ada_b).reshape(depth, SUBLANE, N_MOD, d)

    cos_tab, sin_tab = _rope_tables(t_len, ctx_len)
    n_keys = t_len + ctx_len
    tk = _kv_tile(n_keys)
    tq = ROW_TILE
    zeros = jnp.zeros

    a_cols, b_cols, c_cols = 2 * g, 2 * g, 3 * g + 256
    w_a = w_in[:, :, :a_cols].astype(BF16)
    w_b = w_in[:, :, a_cols:a_cols + b_cols].astype(BF16)
    w_c = w_in[:, :, a_cols + b_cols:a_cols + b_cols + c_cols].astype(BF16)
    w_dq = w_in[:, :, a_cols + b_cols + c_cols:]
    w_kr = w_dq[:, :, Q_LORA + KV_LORA:]
    z64 = zeros((depth, d, QK_NOPE), F32)
    z32 = zeros((depth, d, LANE - QK_NOPE - QK_ROPE), F32)
    w_d = jnp.concatenate([w_dq[:, :, :Q_LORA + KV_LORA], z64, w_kr, z32, z64, _rot_cols(w_kr), z32],
                          axis=2).astype(BF16)

    hw = MLA_HEADS * LANE
    wq = mla_w_uq.reshape(depth, Q_LORA, MLA_HEADS, QK_NOPE + QK_ROPE)
    zq = zeros((depth, Q_LORA, MLA_HEADS, LANE - QK_NOPE - QK_ROPE), F32)
    wqm = jnp.concatenate([wq, zq], axis=-1).reshape(depth, Q_LORA, hw).astype(BF16)
    wqr = jnp.concatenate([zeros((depth, Q_LORA, MLA_HEADS, QK_NOPE), F32), _rot_cols(wq[..., QK_NOPE:]), zq],
                          axis=-1).reshape(depth, Q_LORA, hw).astype(BF16)
    wkv = mla_w_ukv.reshape(depth, KV_LORA, MLA_HEADS, QK_NOPE + V_HEAD)
    zk = zeros((depth, KV_LORA, MLA_HEADS, LANE - QK_NOPE), F32)
    wk = jnp.concatenate([wkv[..., :QK_NOPE], zk], axis=-1).reshape(depth, KV_LORA, hw).astype(BF16)
    wv = jnp.concatenate([wkv[..., QK_NOPE:], zk], axis=-1).reshape(depth, KV_LORA, hw).astype(BF16)

    wo_d = w_out[:, 3 * g:].reshape(depth, MLA_HEADS, V_HEAD, d)
    wo_d = jnp.concatenate([wo_d, zeros((depth, MLA_HEADS, LANE - V_HEAD, d), F32)], axis=2)
    wo_parts = [w_out[:, 0:g].astype(BF16), w_out[:, g:2 * g].astype(BF16), w_out[:, 2 * g:3 * g].astype(BF16),
                wo_d.reshape(depth, hw, d).astype(BF16)]

    lru_gate_w = jnp.concatenate([_block_diag(lru_wa), _block_diag(lru_wx)], axis=-1).astype(BF16)
    lru_gate_b = jnp.concatenate([lru_ba, lru_bx], axis=-1)
    ffn_w = [(ffn1_w13.astype(BF16), ffn1_w2.astype(BF16)), (ffn2_w13.astype(BF16), ffn2_w2.astype(BF16))]

    ctx_flat = ctx.reshape(ctx_rows, d)
    x_flat = x.reshape(n_batch * t_len, d)
    xs = None
    out = None
    for l in range(depth):
        mods = mods_all[l]
        last = l == depth - 1
        rwkv_p = dict(mu_prev=rwkv_mu_prev[l], mu_next=rwkv_mu_next[l], w0=rwkv_w0[l], w_up=rwkv_w_up[l],
                      a0=rwkv_a0[l], a_up=rwkv_a_up[l], g_up=rwkv_g_up[l], k_k=rwkv_k_k[l], k_a=rwkv_k_a[l],
                      r_k=rwkv_r_k[l].reshape(g), gn_g=rwkv_gn_g[l], gn_b=rwkv_gn_b[l])

        xs = _ffn_call((ctx_flat, x_flat) if l == 0 else xs, mods, *ffn_w[0], layer=l, row0=0,
                       tiles_per_batch=tpb, ctx_rows=ctx_rows)

        u_a, u_b, u_c, u_d = _win_call(xs, mods, w_a, w_b, w_c, w_d, layer=l, n_batch=n_batch, cpb=cpb)

        y_a = _conv_call(u_a, cv_dw_w[l], cv_dw_b[l], cv_ln_g[l], cv_ln_b[l], n_batch=n_batch, cpb=cpb)

        h_rev = _lru_call(u_b, lru_conv_w[l], lru_conv_b[l], lru_gate_w[l, 1], lru_gate_b[l, 1], lru_lambda[l, 1],
                          None, n_batch=n_batch, cpb=cpb, reverse=True)
        y_b = _lru_call(u_b, lru_conv_w[l], lru_conv_b[l], lru_gate_w[l, 0], lru_gate_b[l, 0], lru_lambda[l, 0],
                        h_rev, n_batch=n_batch, cpb=cpb, reverse=False)

        y_rev, bonus_rev = _rwkv_call(u_c, rwkv_p, None, None, n_batch=n_batch, cpb=cpb, reverse=True)
        y_c = _rwkv_call(u_c, rwkv_p, y_rev, bonus_rev, n_batch=n_batch, cpb=cpb, reverse=False)

        q, k, v = _mla_proj_call(u_d, cos_tab, sin_tab, mla_q_norm[l], mla_kv_norm[l], wqm[l], wqr[l], wk[l],
                                 wv[l], n_batch=n_batch, cpb=cpb)
        q_off = ctx_rows // tq
        o_lat = _attn_call(q, k, v, out_rows=n_batch * t_len, n_batch=n_batch, tq=tq, tk=tk,
                           n_q=t_len // tq, n_kv=n_keys // tk,
                           q_block=lambda b, qi: q_off + b * (t_len // tq) + qi,
                           kv_block=lambda b, kj: b * (n_keys // tk) + kj,
                           o_block=lambda b, qi: b * (t_len // tq) + qi, name="mla_attn")
        o_ctx = None if last else _attn_call(
            q, k, v, out_rows=ctx_rows, n_batch=n_batch, tq=CHUNK, tk=CHUNK, n_q=1, n_kv=1,
            q_block=lambda b, qi: b, kv_block=lambda b, kj: b * (cpb + 1) + cpb,
            o_block=lambda b, qi: b, name="mla_attn_ctx")

        mix = (y_a, y_b, y_c, o_lat, o_ctx, wo_parts)
        if last:
            out = _ffn_call(xs, mods, *ffn_w[1], layer=l, row0=6, tiles_per_batch=tpb, ctx_rows=ctx_rows,
                            mix=mix, final_gain=final_norm)
        else:
            xs = _ffn_call(xs, mods, *ffn_w[1], layer=l, row0=6, tiles_per_batch=tpb, ctx_rows=ctx_rows, mix=mix)

    return out.reshape(n_batch, t_len, d)
```

```python
import functools

import jax
import jax.numpy as jnp
from jax import lax
from jax.experimental import pallas as pl
from jax.experimental.pallas import tpu as pltpu

F32 = jnp.float32
BF16 = jnp.bfloat16

GRID_W = 64
N_MOD = 9
EPS = 1e-6
GROUP_W = 256
CONV_K = 31
LN_EPS = 1e-5
LRU_CONV_K = 4
LRU_C = 8.0
RWKV_HEAD = 64
RWKV_HEADS = 4
GN_EPS = 64e-5
QK_NOPE = 64
QK_ROPE = 32
V_HEAD = 64
MLA_HEADS = 4
Q_LORA = 256
KV_LORA = 128
ROPE_BASE = 10000.0
SM_SCALE = (QK_NOPE + QK_ROPE) ** -0.5
LOG2_E = 1.4426950408889634

LANE = 128
SUBLANE = 8
MXU_TILE = 256
ROW_TILE = 512
CHUNK = 256
SUB = 64
CONV_HALO = 16
ATTN_MAX_KEYS = 4224
ATTN_UNIT_KEYS = 4224
VMEM_LIMIT = 48 * 1024 * 1024


def _cparams(sem):
    return pltpu.CompilerParams(dimension_semantics=sem, vmem_limit_bytes=VMEM_LIMIT)


def _sigmoid(x):
    return 1.0 / (1.0 + jnp.exp(-x))


def _softplus(x):
    return jnp.maximum(x, 0.0) + jnp.log1p(jnp.exp(-jnp.abs(x)))


def _gelu_tanh(x):
    return 0.5 * x * (1.0 + jnp.tanh(0.7978845608028654 * (x + 0.044715 * (x * x * x))))


def _mm(a, b):
    return jnp.dot(a.astype(BF16), b.astype(BF16), preferred_element_type=F32)


def _mm_nt(a, b):
    return lax.dot_general(a.astype(BF16), b.astype(BF16), (((1,), (1,)), ((), ())),
                           preferred_element_type=F32)


def _split2(a):
    hi = a.astype(BF16)
    lo = (a - hi.astype(F32)).astype(BF16)
    return hi, lo


def _split3(a):
    hi = a.astype(BF16)
    r1 = a - hi.astype(F32)
    mid = r1.astype(BF16)
    lo = (r1 - mid.astype(F32)).astype(BF16)
    return hi, mid, lo


def _mm_exact_rhs(a, b_exact):
    hi, lo = _split2(a)
    d = functools.partial(jnp.dot, preferred_element_type=F32)
    return d(hi, b_exact) + d(lo, b_exact)


def _mm_exact_lhs(a_exact, b):
    hi, mid, lo = _split3(b)
    d = functools.partial(jnp.dot, preferred_element_type=F32)
    return d(a_exact, hi) + d(a_exact, mid) + d(a_exact, lo)


def _mm3(a, b):
    ah, al = _split2(a)
    bh, bl = _split2(b)
    d = functools.partial(jnp.dot, preferred_element_type=F32)
    return d(ah, bh) + d(ah, bl) + d(al, bh)


def _mm3_nt(a, b):
    ah, al = _split2(a)
    bh, bl = _split2(b)
    d = functools.partial(lax.dot_general, dimension_numbers=(((1,), (1,)), ((), ())),
                          preferred_element_type=F32)
    return d(ah, bh) + d(ah, bl) + d(al, bh)


def _mm_tn(a, b):
    return lax.dot_general(a.astype(BF16), b.astype(BF16), (((0,), (0,)), ((), ())),
                           preferred_element_type=F32)


def _modulate(x, shift, scale):
    ms = jnp.mean(x * x, axis=-1, keepdims=True)
    return x * lax.rsqrt(ms + EPS) * (1.0 + scale) + shift


def _ada_kernel(c_ref, w_ref, b_ref, o_ref):
    c = c_ref[...]
    s = c * _sigmoid(c)
    o_ref[...] = _mm(s, w_ref[...]) + b_ref[...]


def _ada_call(cvec, ada_w, ada_b):
    n_layers, d, nd = ada_w.shape
    tn = nd // 8
    return pl.pallas_call(
        _ada_kernel,
        out_shape=jax.ShapeDtypeStruct((n_layers, SUBLANE, nd), F32),
        grid=(n_layers, nd // tn),
        in_specs=[
            pl.BlockSpec((SUBLANE, d), lambda l, j: (0, 0)),
            pl.BlockSpec((None, d, tn), lambda l, j: (l, 0, j)),
            pl.BlockSpec((None, 1, tn), lambda l, j: (l, 0, j)),
        ],
        out_specs=pl.BlockSpec((None, SUBLANE, tn), lambda l, j: (l, 0, j)),
        compiler_params=_cparams(("parallel", "parallel")),
        name="ada_mod",
    )(cvec, ada_w, ada_b.reshape(n_layers, 1, nd))


def _ffn_kernel(*refs, row0, d_ff, chunks, split_input, mix, ctx_mix, final):
    refs = list(refs)
    o_ref = refs.pop()
    i = pl.program_id(0)
    if split_input:
        ctx_ref, lat_ref = refs.pop(0), refs.pop(0)
        x = jnp.where(i == 0, ctx_ref[...], lat_ref[...])
    else:
        x = refs.pop(0)[...]
    mod_ref = refs.pop(0)
    if mix:
        ya_ref, yb0_ref, yb1_ref, yc0_ref, yc1_ref, yd_ref = (refs.pop(0) for _ in range(6))
        yb = jnp.concatenate([yb0_ref[...], yb1_ref[...]], axis=0)
        yc = jnp.concatenate([yc0_ref[...], yc1_ref[...]], axis=0)
        yd = yd_ref[...]
        if ctx_mix:
            yd = jnp.where(i == 0, refs.pop(0)[...], yd)
        wa_ref, wb_ref, wc_ref, wd_ref = (refs.pop(0) for _ in range(4))
        d = functools.partial(jnp.dot, preferred_element_type=F32)
        y = (d(ya_ref[...], wa_ref[...]) + d(yb, wb_ref[...])
             + d(yc, wc_ref[...]) + d(yd, wd_ref[...]))
        x = x + mod_ref[5:6, :] * y
    w13_ref, w2_ref = refs.pop(0), refs.pop(0)
    xm = _modulate(x, mod_ref[row0:row0 + 1, :], mod_ref[row0 + 1:row0 + 2, :]).astype(BF16)
    acc = None
    for lo, hi in chunks:
        g = jnp.dot(xm, w13_ref[:, lo:hi], preferred_element_type=F32)
        u = jnp.dot(xm, w13_ref[:, d_ff + lo:d_ff + hi], preferred_element_type=F32)
        a = ((g * _sigmoid(g)) * u).astype(BF16)
        part = jnp.dot(a, w2_ref[lo:hi, :], preferred_element_type=F32)
        acc = part if acc is None else acc + part
    out = x + 0.5 * mod_ref[row0 + 2:row0 + 3, :] * acc
    if final:
        gain_ref = refs.pop(0)
        out = out * lax.rsqrt(jnp.mean(out * out, axis=-1, keepdims=True) + EPS) * gain_ref[...]
    o_ref[...] = out


def _group_of_tile(i, tiles_per_batch):
    return (i + tiles_per_batch - 1) // tiles_per_batch


def _ffn_call(x, mods, w13, w2, *, layer, row0, tiles_per_batch, ctx_rows, mix=None, final_gain=None):
    split_input = isinstance(x, tuple)
    final = final_gain is not None
    assert not (split_input and final)
    d = w13.shape[1]
    d_ff = w2.shape[1]
    n = ctx_rows + x[1].shape[0] if split_input else x.shape[0]
    skip = ctx_rows // ROW_TILE if final else 0
    assert d_ff % MXU_TILE == 0
    n_tiles = d_ff // MXU_TILE
    split = ((n_tiles + 1) // 2) * MXU_TILE
    chunks = ((0, split), (split, d_ff))
    resident = pl.Buffered(1)
    lat_tile = lambda i: jnp.maximum(i + skip - ctx_rows // ROW_TILE, 0)
    row_spec = lambda w: pl.BlockSpec((ROW_TILE, w), lambda i: (i + skip, 0))
    weight_spec = lambda w: pl.BlockSpec((None,) + w.shape[1:], lambda i: (layer, 0, 0), pipeline_mode=resident)

    in_specs, args = [], []
    if split_input:
        in_specs += [pl.BlockSpec((ROW_TILE, d), lambda i: (0, 0)),
                     pl.BlockSpec((ROW_TILE, d), lambda i: (lat_tile(i), 0))]
        args += list(x)
    else:
        in_specs.append(row_spec(d))
        args.append(x)
    in_specs.append(pl.BlockSpec((None, N_MOD, d),
                                 lambda i: (_group_of_tile(i + skip, tiles_per_batch), 0, 0)))
    args.append(mods)
    ctx_mix = False
    if mix is not None:
        ya, yb, yc, yd_lat, yd_ctx, wo_parts = mix
        assert ROW_TILE == 2 * CHUNK
        n_b = ctx_rows // CHUNK
        cpb = tiles_per_batch * (ROW_TILE // CHUNK)
        pb_half = lambda y, half: pl.BlockSpec(
            (CHUNK, y.shape[1]), lambda i: (_pb_chunk(2 * (i + skip) + half, n_b, cpb), 0))
        in_specs += [row_spec(ya.shape[1]), pb_half(yb, 0), pb_half(yb, 1), pb_half(yc, 0), pb_half(yc, 1),
                     pl.BlockSpec((ROW_TILE, yd_lat.shape[1]), lambda i: (lat_tile(i), 0))]
        args += [ya, yb, yb, yc, yc, yd_lat]
        ctx_mix = yd_ctx is not None
        if ctx_mix:
            in_specs.append(pl.BlockSpec((ROW_TILE, yd_ctx.shape[1]), lambda i: (0, 0)))
            args.append(yd_ctx)
        in_specs += [weight_spec(w) for w in wo_parts]
        args += list(wo_parts)
    in_specs += [weight_spec(w13), weight_spec(w2)]
    args += [w13, w2]
    if final:
        in_specs.append(pl.BlockSpec((1, d), lambda i: (0, 0)))
        args.append(final_gain.reshape(1, d))
    kern = functools.partial(_ffn_kernel, row0=row0, d_ff=d_ff, chunks=chunks, split_input=split_input,
                             mix=mix is not None, ctx_mix=ctx_mix, final=final)
    out_rows = n - skip * ROW_TILE
    return pl.pallas_call(
        kern,
        out_shape=jax.ShapeDtypeStruct((out_rows, d), F32),
        grid=(out_rows // ROW_TILE,),
        in_specs=in_specs,
        out_specs=pl.BlockSpec((ROW_TILE, d), lambda i: (i, 0)),
        compiler_params=_cparams(("parallel",)),
        name="ffn_mix" if mix is not None else "ffn",
    )(*args)


def _win_kernel(x_ref, mod_ref, wa_ref, wb_ref, wc_ref, wd_ref, oa_ref, ob_ref, oc_ref, od_ref):
    xm = _modulate(x_ref[...], mod_ref[3:4, :], mod_ref[4:5, :]).astype(BF16)
    oa_ref[...] = jnp.dot(xm, wa_ref[...], preferred_element_type=F32)
    ob_ref[...] = jnp.dot(xm, wb_ref[...], preferred_element_type=F32)
    oc_ref[...] = jnp.dot(xm, wc_ref[...], preferred_element_type=F32)
    od_ref[...] = jnp.dot(xm, wd_ref[...], preferred_element_type=F32)


def _win_call(x, mods, wa, wb, wc, wd, *, layer, n_batch, cpb):
    n, d = x.shape
    widths = (wa.shape[2], wb.shape[2], wc.shape[2], wd.shape[2])
    group = lambda c: jnp.where(c < n_batch, 0, 1 + jnp.maximum(c - n_batch, 0) // cpb)
    flat = lambda c: (c, 0)
    per_batch = lambda c: (_pb_chunk(c, n_batch, cpb), 0)
    out_maps = (flat, per_batch, per_batch, flat)
    return pl.pallas_call(
        _win_kernel,
        out_shape=tuple(jax.ShapeDtypeStruct((n, w), F32) for w in widths),
        grid=(n // CHUNK,),
        in_specs=[
            pl.BlockSpec((CHUNK, d), flat),
            pl.BlockSpec((None, N_MOD, d), lambda c: (group(c), 0, 0)),
        ] + [pl.BlockSpec((None, d, w), lambda c: (layer, 0, 0)) for w in widths],
        out_specs=tuple(pl.BlockSpec((CHUNK, w), m) for w, m in zip(widths, out_maps)),
        compiler_params=_cparams(("parallel",)),
        name="w_in",
    )(x, mods, wa, wb, wc, wd)


def _seq_flags(c, n_batch, cpb):
    j = lax.rem(jnp.maximum(c - n_batch, 0), cpb)
    is_ctx = c < n_batch
    first = jnp.logical_or(is_ctx, j == 0)
    last = jnp.logical_or(is_ctx, j == cpb - 1)
    return first, last


def _pb_chunk(c, n_batch, cpb):
    lat = jnp.maximum(c - n_batch, 0)
    b = lat // cpb
    return jnp.where(c < n_batch, c * (cpb + 1), b * (cpb + 1) + 1 + (lat - b * cpb))


def _pb_scan_chunk(i, cpb, reverse):
    if reverse:
        return jnp.where(i == 0, 0, cpb + 1 - i)
    return i


def _conv_kernel(cur_ref, prev_ref, next_ref, w_ref, b_ref, g_ref, beta_ref, o_ref, zbuf, zrot,
                 *, n_batch, cpb):
    c = pl.program_id(0)
    first, last = _seq_flags(c, n_batch, cpb)

    def glu(u):
        return u[:, :GROUP_W] * _sigmoid(u[:, GROUP_W:])

    zp = glu(prev_ref[...])
    zn = glu(next_ref[...])
    zbuf[0:CONV_HALO, :] = jnp.where(first, 0.0, zp)
    zbuf[CONV_HALO:CONV_HALO + CHUNK, :] = glu(cur_ref[...])
    zbuf[CONV_HALO + CHUNK:2 * CONV_HALO + CHUNK, :] = jnp.where(last, 0.0, zn)

    pad = CONV_K // 2
    span = CHUNK + 2 * CONV_HALO - SUBLANE
    acc = jnp.zeros((CHUNK, GROUP_W), F32) + b_ref[...]
    for phase in range(SUBLANE):
        if phase:
            zrot[phase - 1] = zbuf[phase:phase + span, :]
        for j in range(CONV_K):
            start = CONV_HALO - pad + j
            if start % SUBLANE == phase:
                lo = start - phase
                win = zrot[phase - 1, lo:lo + CHUNK, :] if phase else zbuf[lo:lo + CHUNK, :]
                acc = acc + w_ref[j:j + 1, :] * win
    mu = jnp.mean(acc, axis=-1, keepdims=True)
    xc = acc - mu
    var = jnp.mean(xc * xc, axis=-1, keepdims=True)
    y = xc * lax.rsqrt(var + LN_EPS) * g_ref[...] + beta_ref[...]
    o_ref[...] = (y * _sigmoid(y)).astype(o_ref.dtype)


def _conv_call(u_a, dw_w, dw_b, ln_g, ln_b, *, n_batch, cpb):
    n = u_a.shape[0]
    n_chunks = n // CHUNK
    hpc = CHUNK // CONV_HALO
    n_halo = n // CONV_HALO
    w_pad = jnp.zeros((32, GROUP_W), F32).at[:CONV_K].set(dw_w)
    row = lambda v: v.reshape(1, GROUP_W)
    const = lambda c: (0, 0)
    kern = functools.partial(_conv_kernel, n_batch=n_batch, cpb=cpb)
    return pl.pallas_call(
        kern,
        out_shape=jax.ShapeDtypeStruct((n, GROUP_W), BF16),
        grid=(n_chunks,),
        in_specs=[
            pl.BlockSpec((CHUNK, 2 * GROUP_W), lambda c: (c, 0)),
            pl.BlockSpec((CONV_HALO, 2 * GROUP_W), lambda c: (jnp.maximum(c * hpc - 1, 0), 0)),
            pl.BlockSpec((CONV_HALO, 2 * GROUP_W),
                         lambda c: (jnp.minimum((c + 1) * hpc, n_halo - 1), 0)),
            pl.BlockSpec((32, GROUP_W), const),
            pl.BlockSpec((1, GROUP_W), const),
            pl.BlockSpec((1, GROUP_W), const),
            pl.BlockSpec((1, GROUP_W), const),
        ],
        out_specs=pl.BlockSpec((CHUNK, GROUP_W), lambda c: (c, 0)),
        scratch_shapes=[pltpu.VMEM((CHUNK + 2 * CONV_HALO, GROUP_W), F32),
                        pltpu.VMEM((SUBLANE - 1, CHUNK + 2 * CONV_HALO - SUBLANE, GROUP_W), F32)],
        compiler_params=_cparams(("parallel",)),
        name="conformer_conv",
    )(u_a, u_a, u_a, w_pad, row(dw_b), row(ln_g), row(ln_b))


def _lru_kernel(*refs, n_batch, cpb, reverse):
    if reverse:
        cur_ref, prev_ref, next_ref, cw_ref, cb_ref, wg_ref, bg_ref, lam_ref, o_ref, xbuf, carry = refs
        hrev_ref = None
    else:
        (cur_ref, prev_ref, next_ref, cw_ref, cb_ref, wg_ref, bg_ref, lam_ref, hrev_ref,
         o_ref, xbuf, carry) = refs
    i = pl.program_id(0)
    jc = _pb_scan_chunk(i, cpb, reverse)
    first = jnp.logical_or(i == 0, jc == 1)
    last = jnp.logical_or(i == 0, jc == cpb)
    rows_all = n_batch * CHUNK

    @pl.when(i == 0)
    def _():
        carry[...] = jnp.zeros_like(carry)

    u = cur_ref[...]
    gb = u[:, :, GROUP_W:].reshape(rows_all, GROUP_W)
    xbuf[:, 0:SUBLANE, :] = jnp.where(first, 0.0, prev_ref[:, :, :GROUP_W])
    xbuf[:, SUBLANE:SUBLANE + CHUNK, :] = u[:, :, :GROUP_W]
    xbuf[:, SUBLANE + CHUNK:2 * SUBLANE + CHUNK, :] = jnp.where(last, 0.0, next_ref[:, :, :GROUP_W])
    pad_l = LRU_CONV_K // 2
    xv = jnp.zeros((rows_all, GROUP_W), F32) + cb_ref[...]
    for j in range(LRU_CONV_K):
        start = SUBLANE - pad_l + j
        xv = xv + cw_ref[j:j + 1, :] * xbuf[:, start:start + CHUNK, :].reshape(rows_all, GROUP_W)

    z = _mm(xv, wg_ref[...]) + bg_ref[...]
    r = _sigmoid(z[:, :GROUP_W])
    ig = _sigmoid(z[:, GROUP_W:])
    log_a = (-LRU_C) * r * _softplus(-lam_ref[...])
    a = jnp.exp(log_a)
    t = jnp.tanh(log_a)
    bb = jnp.sqrt(-2.0 * t / (1.0 - t)) * (ig * xv)

    row = lax.rem(lax.broadcasted_iota(jnp.int32, (rows_all, GROUP_W), 0), CHUNK)
    s = 1
    while s < CHUNK:
        if reverse:
            a_sh = pltpu.roll(a, rows_all - s, axis=0)
            b_sh = pltpu.roll(bb, rows_all - s, axis=0)
            valid = row < CHUNK - s
        else:
            a_sh = pltpu.roll(a, s, axis=0)
            b_sh = pltpu.roll(bb, s, axis=0)
            valid = row >= s
        bb = jnp.where(valid, a * b_sh + bb, bb)
        a = jnp.where(valid, a * a_sh, a)
        s *= 2
    carried = jnp.concatenate([jnp.broadcast_to(carry[bi, 0:1, :], (CHUNK, GROUP_W)) for bi in range(n_batch)],
                              axis=0)
    h = bb + a * carried
    for bi in range(n_batch):
        e = bi * CHUNK if reverse else (bi + 1) * CHUNK - 1
        carry[bi] = jnp.broadcast_to(h[e:e + 1, :], carry.shape[1:])

    if reverse:
        o_ref[...] = h.reshape(n_batch, CHUNK, GROUP_W)
    else:
        out = (h + hrev_ref[...].reshape(rows_all, GROUP_W)) * _gelu_tanh(gb)
        o_ref[...] = out.astype(o_ref.dtype).reshape(n_batch, CHUNK, GROUP_W)


def _lru_call(u_b, conv_w, conv_b, w_gate, b_gate, lam, h_rev, *, n_batch, cpb, reverse):
    n, wb = u_b.shape
    bpc = CHUNK // SUBLANE
    cps = cpb + 1
    assert n == n_batch * cps * CHUNK
    chunk = lambda i: _pb_scan_chunk(i, cpb, reverse)
    const = lambda i: (0, 0)
    u4 = u_b.reshape(n_batch, cps, CHUNK, wb)
    u8 = u_b.reshape(n_batch, cps * bpc, SUBLANE, wb)
    cw_pad = jnp.concatenate([conv_w, jnp.zeros((SUBLANE - LRU_CONV_K, GROUP_W), F32)], axis=0)
    seq_spec = pl.BlockSpec((n_batch, None, CHUNK, GROUP_W), lambda i: (0, chunk(i), 0, 0))
    in_specs = [
        pl.BlockSpec((n_batch, None, CHUNK, wb), lambda i: (0, chunk(i), 0, 0)),
        pl.BlockSpec((n_batch, None, SUBLANE, wb), lambda i: (0, jnp.maximum(chunk(i) * bpc - 1, 0), 0, 0)),
        pl.BlockSpec((n_batch, None, SUBLANE, wb),
                     lambda i: (0, jnp.minimum((chunk(i) + 1) * bpc, cps * bpc - 1), 0, 0)),
        pl.BlockSpec((SUBLANE, GROUP_W), const),
        pl.BlockSpec((1, GROUP_W), const),
        pl.BlockSpec((GROUP_W, 2 * GROUP_W), const),
        pl.BlockSpec((1, 2 * GROUP_W), const),
        pl.BlockSpec((1, GROUP_W), const),
    ]
    args = [u4, u8, u8, cw_pad, conv_b.reshape(1, GROUP_W), w_gate, b_gate.reshape(1, 2 * GROUP_W),
            lam.reshape(1, GROUP_W)]
    if not reverse:
        in_specs.append(seq_spec)
        args.append(h_rev)
    kern = functools.partial(_lru_kernel, n_batch=n_batch, cpb=cpb, reverse=reverse)
    out = pl.pallas_call(
        kern,
        out_shape=jax.ShapeDtypeStruct((n_batch, cps, CHUNK, GROUP_W), F32 if reverse else BF16),
        grid=(cps,),
        in_specs=in_specs,
        out_specs=seq_spec,
        scratch_shapes=[pltpu.VMEM((n_batch, CHUNK + 2 * SUBLANE, GROUP_W), F32),
                        pltpu.VMEM((n_batch, SUBLANE, GROUP_W), F32)],
        compiler_params=_cparams(("arbitrary",)),
        name="rglru_rev" if reverse else "rglru_fwd",
    )(*args)
    return out if reverse else out.reshape(n, GROUP_W)


def _rwkv_kernel(*refs, n_batch, cpb, reverse):
    if reverse:
        (cur_ref, prev_ref, next_ref, mup_ref, mun_ref, w0_ref, wup_ref, a0_ref, aup_ref, gup_ref,
         kk_ref, ka_ref, rk_ref, gng_ref, gnb_ref, hones_ref,
         y_ref, bonus_ref, ubuf, st_scr) = refs
        yrev_ref = brev_ref = None
    else:
        (cur_ref, prev_ref, next_ref, mup_ref, mun_ref, w0_ref, wup_ref, a0_ref, aup_ref, gup_ref,
         kk_ref, ka_ref, rk_ref, gng_ref, gnb_ref, hones_ref, yrev_ref, brev_ref,
         y_ref, ubuf, st_scr) = refs
    i = pl.program_id(0)
    j = _pb_scan_chunk(i, cpb, reverse)
    first = jnp.logical_or(i == 0, j == 1)
    last = jnp.logical_or(i == 0, j == cpb)
    g4 = RWKV_HEADS * RWKV_HEAD
    assert g4 == GROUP_W and RWKV_HEADS * SUB == GROUP_W
    rows_all = n_batch * CHUNK
    wc = cur_ref.shape[-1]

    @pl.when(i == 0)
    def _():
        st_scr[...] = jnp.zeros_like(st_scr)

    ubuf[:, 0:SUBLANE, :] = jnp.where(first, 0.0, prev_ref[...])
    ubuf[:, SUBLANE:SUBLANE + CHUNK, :] = cur_ref[...]
    ubuf[:, SUBLANE + CHUNK:2 * SUBLANE + CHUNK, :] = jnp.where(last, 0.0, next_ref[...])
    u = cur_ref[...].reshape(rows_all, wc)
    up = ubuf[:, SUBLANE - 1:SUBLANE - 1 + CHUNK, :].reshape(rows_all, wc)
    un = ubuf[:, SUBLANE + 1:SUBLANE + 1 + CHUNK, :].reshape(rows_all, wc)
    vs = u + mup_ref[...] * (up - u) + mun_ref[...] * (un - u)

    hones = hones_ref[...]
    r = vs[:, 0:GROUP_W]
    k = vs[:, GROUP_W:2 * GROUP_W]
    val = vs[:, 2 * GROUP_W:3 * GROUP_W]
    lora_in = vs[:, 3 * GROUP_W:3 * GROUP_W + LANE]
    gate_in = vs[:, 3 * GROUP_W + LANE:]
    w_lin = _mm(jnp.tanh(lora_in), wup_ref[...])
    a_lin = _mm(lora_in, aup_ref[...])
    kq = k * kk_ref[...]
    ss = _mm_exact_rhs(kq * kq, hones)
    kk = kq * lax.rsqrt(jnp.maximum(ss, 1e-24))
    lw = -jnp.exp(-_softplus(-(w0_ref[...] + w_lin)) - 0.5)
    a = _sigmoid(a0_ref[...] + a_lin)
    kd = k * (1.0 + (a - 1.0) * ka_ref[...])
    bonus = _mm_exact_rhs(r * kd * rk_ref[...], hones) * val

    bvec = kk * a

    ri = lax.broadcasted_iota(jnp.int32, (g4, g4), 0)
    ci = lax.broadcasted_iota(jnp.int32, (g4, g4), 1)
    same = (ri // SUB) == (ci // SUB)
    rt = lax.rem(ri, SUB)
    ct = lax.rem(ci, SUB)
    if reverse:
        strict = jnp.logical_and(same, ct > rt)
        incl = jnp.logical_and(same, ct >= rt)
    else:
        strict = jnp.logical_and(same, ct < rt)
        incl = jnp.logical_and(same, ct <= rt)
    eye = ri == ci
    ti = lax.broadcasted_iota(jnp.int32, (SUB, SUB), 0)
    tj = lax.broadcasted_iota(jnp.int32, (SUB, SUB), 1)
    tri = jnp.where((tj >= ti) if reverse else (tj <= ti), 1.0, 0.0).astype(BF16)
    lane_head = lax.broadcasted_iota(jnp.int32, (1, g4), 1) // RWKV_HEAD

    def stack(x):
        return jnp.concatenate([jnp.where(lane_head == h, x, 0.0) for h in range(RWKV_HEADS)], axis=0)

    def unstack(x):
        out = x[0:SUB, :]
        for h in range(1, RWKV_HEADS):
            out = out + x[h * SUB:(h + 1) * SUB, :]
        return out

    per_batch = CHUNK // SUB
    n_sub = n_batch * per_batch
    subs = range(n_sub)
    rows = [slice(sc * SUB, (sc + 1) * SUB) for sc in subs]
    each = lambda f, *lists: [f(*xs) for xs in zip(*lists)]
    lw_s = [lw[rw] for rw in rows]
    kk_s = [kk[rw] for rw in rows]
    b_s = [bvec[rw] for rw in rows]
    kd_s = [kd[rw] for rw in rows]
    r_s = [r[rw] for rw in rows]
    v_s = [val[rw] for rw in rows]

    cum = each(lambda x: _mm_exact_lhs(tri, x), lw_s)
    edge = each(lambda x: x[0:1, :] if reverse else x[SUB - 1:SUB, :], cum)
    a_st = each(lambda cu, l, x: stack(jnp.exp(cu - l) * x), cum, lw_s, kk_s)
    b_st = each(lambda cu, x: stack(x * jnp.exp(-cu)), cum, b_s)
    k_st = each(lambda cu, x: stack(x * jnp.exp(-cu)), cum, kd_s)
    r_st = each(lambda cu, x: stack(x * jnp.exp(cu)), cum, r_s)
    v_bf = each(lambda x: stack(x).astype(BF16), v_s)
    e_out = each(lambda ed, cu: jnp.exp(ed - cu), edge, cum)
    beta_w = each(lambda x, e: x * e, b_s, e_out)
    kappa_w = each(lambda x, e: x * e, kd_s, e_out)
    w_all = each(jnp.exp, edge)

    prod = each(lambda a_, r_, b_, k_: _mm_nt(jnp.concatenate([a_, r_], axis=0),
                                              jnp.concatenate([b_, k_], axis=0)),
                a_st, r_st, b_st, k_st)
    l_ak = each(lambda p_: jnp.where(strict, p_[0:g4, g4:], 0.0), prod)
    m_rb = each(lambda p_: jnp.where(incl, p_[g4:, 0:g4], 0.0).astype(BF16), prod)
    m_rk = each(lambda p_: jnp.where(incl, p_[g4:, g4:], 0.0), prod)

    l_ab = each(lambda p_: jnp.where(strict, p_[0:g4, 0:g4], 0.0), prod)
    q = each(lambda l_: -l_, l_ab)
    tm = q
    for _ in range(5):
        q = each(lambda q_: _mm(q_, q_), q)
        tm = each(lambda t_, q_: t_ + q_ + _mm(t_, q_), tm, q)

    def refine(t_, l_):
        res = -(t_ + l_) - _mm3(l_, t_)
        return t_ + res + _mm(t_, res)

    tm = each(refine, tm, l_ab)

    lakv = each(_mm, l_ak, v_bf)
    rhs = each(lambda a_, x: jnp.concatenate([a_, x], axis=1), a_st, lakv)
    sol = each(lambda t_, x: x + _mm(t_, x), tm, rhs)
    corr = each(_mm, m_rb, sol)
    ra_st = each(lambda r_, c_: r_ - c_[:, 0:g4], r_st, corr)
    y0_st = each(lambda m_, v_, c_: _mm(m_, v_) - c_[:, g4:], m_rk, v_bf, corr)
    ta = each(lambda s_: unstack(s_[:, 0:g4]), sol)
    z0 = each(lambda s_: unstack(s_[:, g4:]), sol)
    gt = each(lambda w_, bw, ta_: jnp.where(eye, w_, 0.0) - jnp.where(same, _mm_tn(bw, ta_), 0.0),
              w_all, beta_w, ta)
    ht = each(lambda kw, v_, bw, z_: jnp.where(same, _mm_tn(kw, v_) - _mm_tn(bw, z_), 0.0),
              kappa_w, v_s, beta_w, z0)

    order = list(range(per_batch - 1, -1, -1)) if reverse else list(range(per_batch))
    st = [st_scr[bi] for bi in range(n_batch)]
    y_parts = {}
    for t in order:
        for bi in range(n_batch):
            sc = bi * per_batch + t
            y_parts[sc] = unstack(_mm(ra_st[sc], st[bi]) + y0_st[sc])
            st[bi] = _mm3(gt[sc], st[bi]) + ht[sc]
    for bi in range(n_batch):
        st_scr[bi] = st[bi]
    y_all = jnp.concatenate([y_parts[sc] for sc in subs], axis=0)

    if reverse:
        y_ref[...] = y_all.reshape(n_batch, CHUNK, GROUP_W)
        bonus_ref[...] = bonus.reshape(n_batch, CHUNK, GROUP_W)
    else:
        yy = y_all + yrev_ref[...].reshape(rows_all, GROUP_W)
        inv_n = 1.0 / RWKV_HEAD
        mu = _mm_exact_rhs(yy, hones) * inv_n
        yc = yy - mu
        var = _mm_exact_rhs(yc * yc, hones) * inv_n
        o = (yc * lax.rsqrt(var + GN_EPS) * gng_ref[...] + gnb_ref[...] + bonus
             + brev_ref[...].reshape(rows_all, GROUP_W))
        gate = _mm(_sigmoid(gate_in), gup_ref[...])
        y_ref[...] = (o * gate).astype(y_ref.dtype).reshape(n_batch, CHUNK, GROUP_W)


def _rwkv_call(u_c, p, y_rev, bonus_rev, *, n_batch, cpb, reverse):
    n, wc = u_c.shape
    bpc = CHUNK // SUBLANE
    cps = cpb + 1
    assert n == n_batch * cps * CHUNK
    chunk = lambda i: _pb_scan_chunk(i, cpb, reverse)
    const = lambda i: (0, 0)
    u4 = u_c.reshape(n_batch, cps, CHUNK, wc)
    u8 = u_c.reshape(n_batch, cps * bpc, SUBLANE, wc)
    d = 1 if reverse else 0
    row = lambda v: v.reshape(1, -1)
    zeros64 = jnp.zeros((RWKV_HEAD, GROUP_W), F32)
    wup = jnp.concatenate([p['w_up'][d], zeros64], axis=0).astype(BF16)
    aup = jnp.concatenate([zeros64, p['a_up'][d]], axis=0).astype(BF16)
    hid = jnp.arange(GROUP_W) // RWKV_HEAD
    hones = (hid[:, None] == hid[None, :]).astype(BF16)
    small = [row(p['mu_prev']), row(p['mu_next']), row(p['w0'][d]), wup, row(p['a0'][d]), aup,
             p['g_up'].astype(BF16), row(p['k_k']), row(p['k_a']), row(p['r_k']), row(p['gn_g']),
             row(p['gn_b']), hones]
    in_specs = [
        pl.BlockSpec((n_batch, None, CHUNK, wc), lambda i: (0, chunk(i), 0, 0)),
        pl.BlockSpec((n_batch, None, SUBLANE, wc), lambda i: (0, jnp.maximum(chunk(i) * bpc - 1, 0), 0, 0)),
        pl.BlockSpec((n_batch, None, SUBLANE, wc),
                     lambda i: (0, jnp.minimum((chunk(i) + 1) * bpc, cps * bpc - 1), 0, 0)),
    ] + [pl.BlockSpec(s.shape, const) for s in small]
    args = [u4, u8, u8] + small
    seq_spec = pl.BlockSpec((n_batch, None, CHUNK, GROUP_W), lambda i: (0, chunk(i), 0, 0))
    seq_shape = (n_batch, cps, CHUNK, GROUP_W)
    scratch = [pltpu.VMEM((n_batch, CHUNK + 2 * SUBLANE, wc), F32),
               pltpu.VMEM((n_batch, GROUP_W, GROUP_W), F32)]
    if reverse:
        out_shape = (jax.ShapeDtypeStruct(seq_shape, F32), jax.ShapeDtypeStruct(seq_shape, F32))
        out_specs = (seq_spec, seq_spec)
    else:
        in_specs += [seq_spec, seq_spec]
        args += [y_rev, bonus_rev]
        out_shape = jax.ShapeDtypeStruct(seq_shape, BF16)
        out_specs = seq_spec
    kern = functools.partial(_rwkv_kernel, n_batch=n_batch, cpb=cpb, reverse=reverse)
    out = pl.pallas_call(
        kern,
        out_shape=out_shape,
        grid=(cps,),
        in_specs=in_specs,
        out_specs=out_specs,
        scratch_shapes=scratch,
        compiler_params=_cparams(("arbitrary",)),
        name="rwkv7_rev" if reverse else "rwkv7_fwd",
    )(*args)
    return out if reverse else out.reshape(n, GROUP_W)


def _mla_proj_kernel(u_ref, cos_ref, sin_ref, qn_ref, kvn_ref, wqm_ref, wqr_ref, wk_ref, wv_ref,
                     q_ref, k_ref, v_ref):
    u = u_ref[...]
    cq = u[:, :Q_LORA]
    ckv = u[:, Q_LORA:Q_LORA + KV_LORA]
    blk_r = u[:, Q_LORA + KV_LORA:Q_LORA + KV_LORA + LANE]
    blk_rr = u[:, Q_LORA + KV_LORA + LANE:]
    cqn = (cq * lax.rsqrt(jnp.mean(cq * cq, axis=-1, keepdims=True) + EPS) * qn_ref[...]).astype(BF16)
    ckvn = (ckv * lax.rsqrt(jnp.mean(ckv * ckv, axis=-1, keepdims=True) + EPS) * kvn_ref[...]).astype(BF16)
    cos_t = cos_ref[...]
    sin_t = sin_ref[...]
    cos4 = jnp.concatenate([cos_t] * MLA_HEADS, axis=1)
    sin4 = jnp.concatenate([sin_t] * MLA_HEADS, axis=1)
    qm = jnp.dot(cqn, wqm_ref[...], preferred_element_type=F32)
    qr = jnp.dot(cqn, wqr_ref[...], preferred_element_type=F32)
    q_ref[...] = ((qm * cos4 + qr * sin4) * (SM_SCALE * LOG2_E)).astype(q_ref.dtype)
    kr = blk_r * cos_t + blk_rr * sin_t
    km = jnp.dot(ckvn, wk_ref[...], preferred_element_type=F32)
    k_ref[...] = (km + jnp.concatenate([kr] * MLA_HEADS, axis=1)).astype(k_ref.dtype)
    lane = lax.broadcasted_iota(jnp.int32, (1, MLA_HEADS * LANE), 1)
    ones_col = jnp.where(lax.rem(lane, LANE) == V_HEAD, 1.0, 0.0)
    v_ref[...] = (jnp.dot(ckvn, wv_ref[...], preferred_element_type=F32) + ones_col).astype(v_ref.dtype)


def _mla_proj_call(u_d, cos_tab, sin_tab, q_norm, kv_norm, wqm, wqr, wk, wv, *, n_batch, cpb):
    n, wd = u_d.shape
    hw = MLA_HEADS * LANE
    kv_cpb = cpb + 1

    def kv_chunk(c):
        lat = c - n_batch
        bb = lat // cpb
        return jnp.where(c < n_batch, c * kv_cpb + cpb, bb * kv_cpb + (lat - bb * cpb))

    def tab_chunk(c):
        return jnp.where(c < n_batch, 0, 1 + lax.rem(jnp.maximum(c - n_batch, 0), cpb))

    const = lambda c: (0, 0)
    return pl.pallas_call(
        _mla_proj_kernel,
        out_shape=(jax.ShapeDtypeStruct((n, hw), BF16), jax.ShapeDtypeStruct((n, hw), BF16),
                   jax.ShapeDtypeStruct((n, hw), BF16)),
        grid=(n // CHUNK,),
        in_specs=[
            pl.BlockSpec((CHUNK, wd), lambda c: (c, 0)),
            pl.BlockSpec((CHUNK, LANE), lambda c: (tab_chunk(c), 0)),
            pl.BlockSpec((CHUNK, LANE), lambda c: (tab_chunk(c), 0)),
            pl.BlockSpec((1, Q_LORA), const),
            pl.BlockSpec((1, KV_LORA), const),
            pl.BlockSpec((Q_LORA, hw), const),
            pl.BlockSpec((Q_LORA, hw), const),
            pl.BlockSpec((KV_LORA, hw), const),
            pl.BlockSpec((KV_LORA, hw), const),
        ],
        out_specs=(pl.BlockSpec((CHUNK, hw), lambda c: (c, 0)),
                   pl.BlockSpec((CHUNK, hw), lambda c: (kv_chunk(c), 0)),
                   pl.BlockSpec((CHUNK, hw), lambda c: (kv_chunk(c), 0))),
        compiler_params=_cparams(("parallel",)),
        name="mla_proj",
    )(u_d, cos_tab, sin_tab, q_norm.reshape(1, Q_LORA), kv_norm.reshape(1, KV_LORA), wqm, wqr, wk, wv)


def _attn_kernel(q_ref, k_ref, v_ref, o_ref, m_scr, acc_scr, *, n_kv, unit_keys):
    kj = pl.program_id(2)

    @pl.when(kj == 0)
    def _():
        m_scr[...] = jnp.full_like(m_scr, -jnp.inf)
        acc_scr[...] = jnp.zeros_like(acc_scr)

    tk = k_ref.shape[0]
    bounds = list(range(0, tk, unit_keys)) + [tk]
    units = [(h, lo, hi) for h in range(MLA_HEADS) for lo, hi in zip(bounds[:-1], bounds[1:])]

    def scores(u):
        h, lo, hi = u
        sl = slice(h * LANE, (h + 1) * LANE)
        return lax.dot_general(q_ref[:, sl], k_ref[lo:hi, sl], (((1,), (1,)), ((), ())),
                               preferred_element_type=F32)

    def softmax_step(u, s):
        h = u[0]
        m_prev = m_scr[h]
        m_new = jnp.maximum(m_prev, jnp.max(s, axis=-1, keepdims=True))
        m_scr[h] = m_new
        return jnp.exp2(s - m_new).astype(BF16), jnp.exp2(m_prev - m_new)

    def weighted_values(u, p, alpha):
        h, lo, hi = u
        sl = slice(h * LANE, (h + 1) * LANE)
        acc_scr[h] = alpha * acc_scr[h] + jnp.dot(p, v_ref[lo:hi, sl], preferred_element_type=F32)

    s_next = scores(units[0])
    pending = None
    for idx, u in enumerate(units):
        s_cur = s_next
        if idx + 1 < len(units):
            s_next = scores(units[idx + 1])
        p, alpha = softmax_step(u, s_cur)
        if pending is not None:
            weighted_values(*pending)
        pending = (u, p, alpha)
    weighted_values(*pending)

    @pl.when(kj == n_kv - 1)
    def _():
        for h in range(MLA_HEADS):
            sl = slice(h * LANE, (h + 1) * LANE)
            acc = acc_scr[h]
            o_ref[:, sl] = (acc / acc[:, V_HEAD:V_HEAD + 1]).astype(o_ref.dtype)


def _attn_call(q, k, v, *, out_rows, n_batch, tq, tk, n_q, n_kv, q_block, kv_block, o_block, name):
    hw = q.shape[1]
    unit = min(ATTN_UNIT_KEYS, tk)
    kern = functools.partial(_attn_kernel, n_kv=n_kv, unit_keys=unit)
    return pl.pallas_call(
        kern,
        out_shape=jax.ShapeDtypeStruct((out_rows, hw), BF16),
        grid=(n_batch, n_q, n_kv),
        in_specs=[
            pl.BlockSpec((tq, hw), lambda b, qi, kj: (q_block(b, qi), 0)),
            pl.BlockSpec((tk, hw), lambda b, qi, kj: (kv_block(b, kj), 0)),
            pl.BlockSpec((tk, hw), lambda b, qi, kj: (kv_block(b, kj), 0)),
        ],
        out_specs=pl.BlockSpec((tq, hw), lambda b, qi, kj: (o_block(b, qi), 0)),
        scratch_shapes=[pltpu.VMEM((MLA_HEADS, tq, 1), F32), pltpu.VMEM((MLA_HEADS, tq, LANE), F32)],
        compiler_params=_cparams(("parallel", "parallel", "arbitrary")),
        name=name,
    )(q, k, v)


def _kv_tile(n_keys):
    best = LANE
    t = LANE
    while t <= ATTN_MAX_KEYS:
        if n_keys % t == 0:
            best = t
        t += LANE
    return best


def _rot_cols(w):
    q = QK_ROPE // 4
    return jnp.concatenate([-w[..., q:2 * q], w[..., 0:q], -w[..., 3 * q:4 * q], w[..., 2 * q:3 * q]],
                           axis=-1)


def _block_diag(w):
    nb, n = w.shape[-3], w.shape[-2]
    eye = jnp.eye(nb, dtype=w.dtype)
    out = eye[:, None, :, None] * w[..., :, :, None, :]
    return out.reshape(w.shape[:-3] + (nb * n, nb * n))


def _rope_tables(t_len, ctx_len):
    t = jnp.arange(t_len, dtype=jnp.int32)
    rows = (t // GRID_W).astype(F32)
    cols = (t % GRID_W).astype(F32)
    n_freq = QK_ROPE // 4
    inv_freq = ROPE_BASE ** (-jnp.arange(n_freq, dtype=F32) / n_freq)
    ang = jnp.stack([rows[:, None] * inv_freq, cols[:, None] * inv_freq], axis=1)
    ang = jnp.concatenate([ang, ang], axis=-1).reshape(t_len, QK_ROPE)
    cos = jnp.concatenate([jnp.ones((ctx_len, QK_ROPE), F32), jnp.cos(ang)], axis=0)
    sin = jnp.concatenate([jnp.zeros((ctx_len, QK_ROPE), F32), jnp.sin(ang)], axis=0)
    n = t_len + ctx_len
    pad = LANE - QK_NOPE - QK_ROPE
    cos_tab = jnp.concatenate([jnp.ones((n, QK_NOPE), F32), cos, jnp.zeros((n, pad), F32)], axis=1)
    sin_tab = jnp.concatenate([jnp.zeros((n, QK_NOPE), F32), sin, jnp.zeros((n, pad), F32)], axis=1)
    return cos_tab, sin_tab


def kernel(x, c, ctx, c_ctx, ada_w, ada_b, ffn1_w13, ffn1_w2, ffn2_w13, ffn2_w2, w_in, w_out,
           cv_dw_w, cv_dw_b, cv_ln_g, cv_ln_b,
           lru_conv_w, lru_conv_b, lru_wa, lru_ba, lru_wx, lru_bx, lru_lambda,
           rwkv_mu_prev, rwkv_mu_next, rwkv_w0, rwkv_w_up, rwkv_a0, rwkv_a_up, rwkv_g_up,
           rwkv_k_k, rwkv_k_a, rwkv_r_k, rwkv_gn_g, rwkv_gn_b,
           mla_q_norm, mla_w_uq, mla_kv_norm, mla_w_ukv, final_norm):
    n_batch, t_len, d = x.shape
    ctx_len = ctx.shape[1]
    depth = ada_w.shape[0]
    assert ctx_len == CHUNK and n_batch * ctx_len == ROW_TILE and t_len % ROW_TILE == 0
    assert c.shape[0] + 1 <= SUBLANE
    ctx_rows = n_batch * ctx_len
    cpb = t_len // CHUNK
    tpb = t_len // ROW_TILE
    g = GROUP_W

    cvec = jnp.concatenate([c_ctx[None, :], c, jnp.zeros((SUBLANE - 1 - n_batch, d), F32)], axis=0)
    mods_all = _ada_call(cvec, ada_w, ada_b).reshape(depth, SUBLANE, N_MOD, d)

    cos_tab, sin_tab = _rope_tables(t_len, ctx_len)
    n_keys = t_len + ctx_len
    tk = _kv_tile(n_keys)
    tq = ROW_TILE
    zeros = jnp.zeros

    a_cols, b_cols, c_cols = 2 * g, 2 * g, 3 * g + 256
    w_a = w_in[:, :, :a_cols].astype(BF16)
    w_b = w_in[:, :, a_cols:a_cols + b_cols].astype(BF16)
    w_c = w_in[:, :, a_cols + b_cols:a_cols + b_cols + c_cols].astype(BF16)
    w_dq = w_in[:, :, a_cols + b_cols + c_cols:]
    w_kr = w_dq[:, :, Q_LORA + KV_LORA:]
    z64 = zeros((depth, d, QK_NOPE), F32)
    z32 = zeros((depth, d, LANE - QK_NOPE - QK_ROPE), F32)
    w_d = jnp.concatenate([w_dq[:, :, :Q_LORA + KV_LORA], z64, w_kr, z32, z64, _rot_cols(w_kr), z32],
                          axis=2).astype(BF16)

    hw = MLA_HEADS * LANE
    wq = mla_w_uq.reshape(depth, Q_LORA, MLA_HEADS, QK_NOPE + QK_ROPE)
    zq = zeros((depth, Q_LORA, MLA_HEADS, LANE - QK_NOPE - QK_ROPE), F32)
    wqm = jnp.concatenate([wq, zq], axis=-1).reshape(depth, Q_LORA, hw).astype(BF16)
    wqr = jnp.concatenate([zeros((depth, Q_LORA, MLA_HEADS, QK_NOPE), F32), _rot_cols(wq[..., QK_NOPE:]), zq],
                          axis=-1).reshape(depth, Q_LORA, hw).astype(BF16)
    wkv = mla_w_ukv.reshape(depth, KV_LORA, MLA_HEADS, QK_NOPE + V_HEAD)
    zk = zeros((depth, KV_LORA, MLA_HEADS, LANE - QK_NOPE), F32)
    wk = jnp.concatenate([wkv[..., :QK_NOPE], zk], axis=-1).reshape(depth, KV_LORA, hw).astype(BF16)
    wv = jnp.concatenate([wkv[..., QK_NOPE:], zk], axis=-1).reshape(depth, KV_LORA, hw).astype(BF16)

    wo_d = w_out[:, 3 * g:].reshape(depth, MLA_HEADS, V_HEAD, d)
    wo_d = jnp.concatenate([wo_d, zeros((depth, MLA_HEADS, LANE - V_HEAD, d), F32)], axis=2)
    wo_parts = [w_out[:, 0:g].astype(BF16), w_out[:, g:2 * g].astype(BF16), w_out[:, 2 * g:3 * g].astype(BF16),
                wo_d.reshape(depth, hw, d).astype(BF16)]

    lru_gate_w = jnp.concatenate([_block_diag(lru_wa), _block_diag(lru_wx)], axis=-1).astype(BF16)
    lru_gate_b = jnp.concatenate([lru_ba, lru_bx], axis=-1)
    ffn_w = [(ffn1_w13.astype(BF16), ffn1_w2.astype(BF16)), (ffn2_w13.astype(BF16), ffn2_w2.astype(BF16))]

    ctx_flat = ctx.reshape(ctx_rows, d)
    x_flat = x.reshape(n_batch * t_len, d)
    xs = None
    out = None
    for l in range(depth):
        mods = mods_all[l]
        last = l == depth - 1
        rwkv_p = dict(mu_prev=rwkv_mu_prev[l], mu_next=rwkv_mu_next[l], w0=rwkv_w0[l], w_up=rwkv_w_up[l],
                      a0=rwkv_a0[l], a_up=rwkv_a_up[l], g_up=rwkv_g_up[l], k_k=rwkv_k_k[l], k_a=rwkv_k_a[l],
                      r_k=rwkv_r_k[l].reshape(g), gn_g=rwkv_gn_g[l], gn_b=rwkv_gn_b[l])

        xs = _ffn_call((ctx_flat, x_flat) if l == 0 else xs, mods, *ffn_w[0], layer=l, row0=0,
                       tiles_per_batch=tpb, ctx_rows=ctx_rows)

        u_a, u_b, u_c, u_d = _win_call(xs, mods, w_a, w_b, w_c, w_d, layer=l, n_batch=n_batch, cpb=cpb)

        y_a = _conv_call(u_a, cv_dw_w[l], cv_dw_b[l], cv_ln_g[l], cv_ln_b[l], n_batch=n_batch, cpb=cpb)

        h_rev = _lru_call(u_b, lru_conv_w[l], lru_conv_b[l], lru_gate_w[l, 1], lru_gate_b[l, 1], lru_lambda[l, 1],
                          None, n_batch=n_batch, cpb=cpb, reverse=True)
        y_b = _lru_call(u_b, lru_conv_w[l], lru_conv_b[l], lru_gate_w[l, 0], lru_gate_b[l, 0], lru_lambda[l, 0],
                        h_rev, n_batch=n_batch, cpb=cpb, reverse=False)

        y_rev, bonus_rev = _rwkv_call(u_c, rwkv_p, None, None, n_batch=n_batch, cpb=cpb, reverse=True)
        y_c = _rwkv_call(u_c, rwkv_p, y_rev, bonus_rev, n_batch=n_batch, cpb=cpb, reverse=False)

        q, k, v = _mla_proj_call(u_d, cos_tab, sin_tab, mla_q_norm[l], mla_kv_norm[l], wqm[l], wqr[l], wk[l],
                                 wv[l], n_batch=n_batch, cpb=cpb)
        q_off = ctx_rows // tq
        o_lat = _attn_call(q, k, v, out_rows=n_batch * t_len, n_batch=n_batch, tq=tq, tk=tk,
                           n_q=t_len // tq, n_kv=n_keys // tk,
                           q_block=lambda b, qi: q_off + b * (t_len // tq) + qi,
                           kv_block=lambda b, kj: b * (n_keys // tk) + kj,
                           o_block=lambda b, qi: b * (t_len // tq) + qi, name="mla_attn")
        o_ctx = None if last else _attn_call(
            q, k, v, out_rows=ctx_rows, n_batch=n_batch, tq=CHUNK, tk=CHUNK, n_q=1, n_kv=1,
            q_block=lambda b, qi: b, kv_block=lambda b, kj: b * (cpb + 1) + cpb,
            o_block=lambda b, qi: b, name="mla_attn_ctx")

        mix = (y_a, y_b, y_c, o_lat, o_ctx, wo_parts)
        if last:
            out = _ffn_call(xs, mods, *ffn_w[1], layer=l, row0=6, tiles_per_batch=tpb, ctx_rows=ctx_rows,
                            mix=mix, final_gain=final_norm)
        else:
            xs = _ffn_call(xs, mods, *ffn_w[1], layer=l, row0=6, tiles_per_batch=tpb, ctx_rows=ctx_rows, mix=mix)

    return out.reshape(n_batch, t_len, d)
```

```python
import functools

import jax
import jax.numpy as jnp
from jax import lax
from jax.experimental import pallas as pl
from jax.experimental.pallas import tpu as pltpu

F32 = jnp.float32
BF16 = jnp.bfloat16

GRID_W = 64
N_MOD = 9
EPS = 1e-6
GROUP_W = 256
CONV_K = 31
LN_EPS = 1e-5
LRU_CONV_K = 4
LRU_C = 8.0
RWKV_HEAD = 64
RWKV_HEADS = 4
GN_EPS = 64e-5
QK_NOPE = 64
QK_ROPE = 32
V_HEAD = 64
MLA_HEADS = 4
Q_LORA = 256
KV_LORA = 128
ROPE_BASE = 10000.0
SM_SCALE = (QK_NOPE + QK_ROPE) ** -0.5
LOG2_E = 1.4426950408889634

LANE = 128
SUBLANE = 8
MXU_TILE = 256
ROW_TILE = 512
CHUNK = 256
SUB = 64
CONV_HALO = 16
ATTN_MAX_KEYS = 4224
ATTN_UNIT_KEYS = 4224
VMEM_LIMIT = 48 * 1024 * 1024


def _cparams(sem):
    return pltpu.CompilerParams(dimension_semantics=sem, vmem_limit_bytes=VMEM_LIMIT)


def _sigmoid(x):
    return 1.0 / (1.0 + jnp.exp(-x))


def _softplus(x):
    return jnp.maximum(x, 0.0) + jnp.log1p(jnp.exp(-jnp.abs(x)))


def _gelu_tanh(x):
    return 0.5 * x * (1.0 + jnp.tanh(0.7978845608028654 * (x + 0.044715 * (x * x * x))))


def _mm(a, b):
    return jnp.dot(a.astype(BF16), b.astype(BF16), preferred_element_type=F32)


def _mm_nt(a, b):
    return lax.dot_general(a.astype(BF16), b.astype(BF16), (((1,), (1,)), ((), ())),
                           preferred_element_type=F32)


def _split2(a):
    hi = a.astype(BF16)
    lo = (a - hi.astype(F32)).astype(BF16)
    return hi, lo


def _split3(a):
    hi = a.astype(BF16)
    r1 = a - hi.astype(F32)
    mid = r1.astype(BF16)
    lo = (r1 - mid.astype(F32)).astype(BF16)
    return hi, mid, lo


def _mm_exact_rhs(a, b_exact):
    hi, lo = _split2(a)
    d = functools.partial(jnp.dot, preferred_element_type=F32)
    return d(hi, b_exact) + d(lo, b_exact)


def _mm_exact_lhs(a_exact, b):
    hi, mid, lo = _split3(b)
    d = functools.partial(jnp.dot, preferred_element_type=F32)
    return d(a_exact, hi) + d(a_exact, mid) + d(a_exact, lo)


def _mm3(a, b):
    ah, al = _split2(a)
    bh, bl = _split2(b)
    d = functools.partial(jnp.dot, preferred_element_type=F32)
    return d(ah, bh) + d(ah, bl) + d(al, bh)


def _mm3_nt(a, b):
    ah, al = _split2(a)
    bh, bl = _split2(b)
    d = functools.partial(lax.dot_general, dimension_numbers=(((1,), (1,)), ((), ())),
                          preferred_element_type=F32)
    return d(ah, bh) + d(ah, bl) + d(al, bh)


def _mm_tn(a, b):
    return lax.dot_general(a.astype(BF16), b.astype(BF16), (((0,), (0,)), ((), ())),
                           preferred_element_type=F32)


def _modulate(x, shift, scale):
    ms = jnp.mean(x * x, axis=-1, keepdims=True)
    return x * lax.rsqrt(ms + EPS) * (1.0 + scale) + shift


def _ada_kernel(c_ref, w_ref, b_ref, o_ref):
    c = c_ref[...]
    s = c * _sigmoid(c)
    o_ref[...] = _mm(s, w_ref[...]) + b_ref[...]


def _ada_call(cvec, ada_w, ada_b):
    n_layers, d, nd = ada_w.shape
    tn = nd // 8
    return pl.pallas_call(
        _ada_kernel,
        out_shape=jax.ShapeDtypeStruct((n_layers, SUBLANE, nd), F32),
        grid=(n_layers, nd // tn),
        in_specs=[
            pl.BlockSpec((SUBLANE, d), lambda l, j: (0, 0)),
            pl.BlockSpec((None, d, tn), lambda l, j: (l, 0, j)),
            pl.BlockSpec((None, 1, tn), lambda l, j: (l, 0, j)),
        ],
        out_specs=pl.BlockSpec((None, SUBLANE, tn), lambda l, j: (l, 0, j)),
        compiler_params=_cparams(("parallel", "parallel")),
        name="ada_mod",
    )(cvec, ada_w, ada_b.reshape(n_layers, 1, nd))


def _ffn_kernel(*refs, row0, d_ff, chunks, split_input, mix, ctx_mix, final):
    refs = list(refs)
    o_ref = refs.pop()
    i = pl.program_id(0)
    if split_input:
        ctx_ref, lat_ref = refs.pop(0), refs.pop(0)
        x = jnp.where(i == 0, ctx_ref[...], lat_ref[...])
    else:
        x = refs.pop(0)[...]
    mod_ref = refs.pop(0)
    if mix:
        ya_ref, yb0_ref, yb1_ref, yc0_ref, yc1_ref, yd_ref = (refs.pop(0) for _ in range(6))
        yb = jnp.concatenate([yb0_ref[...], yb1_ref[...]], axis=0)
        yc = jnp.concatenate([yc0_ref[...], yc1_ref[...]], axis=0)
        yd = yd_ref[...]
        if ctx_mix:
            yd = jnp.where(i == 0, refs.pop(0)[...], yd)
        wa_ref, wb_ref, wc_ref, wd_ref = (refs.pop(0) for _ in range(4))
        d = functools.partial(jnp.dot, preferred_element_type=F32)
        y = (d(ya_ref[...], wa_ref[...]) + d(yb, wb_ref[...])
             + d(yc, wc_ref[...]) + d(yd, wd_ref[...]))
        x = x + mod_ref[5:6, :] * y
    w13_ref, w2_ref = refs.pop(0), refs.pop(0)
    xm = _modulate(x, mod_ref[row0:row0 + 1, :], mod_ref[row0 + 1:row0 + 2, :]).astype(BF16)
    acc = None
    for lo, hi in chunks:
        g = jnp.dot(xm, w13_ref[:, lo:hi], preferred_element_type=F32)
        u = jnp.dot(xm, w13_ref[:, d_ff + lo:d_ff + hi], preferred_element_type=F32)
        a = ((g * _sigmoid(g)) * u).astype(BF16)
        part = jnp.dot(a, w2_ref[lo:hi, :], preferred_element_type=F32)
        acc = part if acc is None else acc + part
    out = x + 0.5 * mod_ref[row0 + 2:row0 + 3, :] * acc
    if final:
        gain_ref = refs.pop(0)
        out = out * lax.rsqrt(jnp.mean(out * out, axis=-1, keepdims=True) + EPS) * gain_ref[...]
    o_ref[...] = out


def _group_of_tile(i, tiles_per_batch):
    return (i + tiles_per_batch - 1) // tiles_per_batch


def _ffn_call(x, mods, w13, w2, *, layer, row0, tiles_per_batch, ctx_rows, mix=None, final_gain=None):
    split_input = isinstance(x, tuple)
    final = final_gain is not None
    assert not (split_input and final)
    d = w13.shape[1]
    d_ff = w2.shape[1]
    n = ctx_rows + x[1].shape[0] if split_input else x.shape[0]
    skip = ctx_rows // ROW_TILE if final else 0
    assert d_ff % MXU_TILE == 0
    n_tiles = d_ff // MXU_TILE
    split = ((n_tiles + 1) // 2) * MXU_TILE
    chunks = ((0, split), (split, d_ff))
    resident = pl.Buffered(1)
    lat_tile = lambda i: jnp.maximum(i + skip - ctx_rows // ROW_TILE, 0)
    row_spec = lambda w: pl.BlockSpec((ROW_TILE, w), lambda i: (i + skip, 0))
    weight_spec = lambda w: pl.BlockSpec((None,) + w.shape[1:], lambda i: (layer, 0, 0), pipeline_mode=resident)

    in_specs, args = [], []
    if split_input:
        in_specs += [pl.BlockSpec((ROW_TILE, d), lambda i: (0, 0)),
                     pl.BlockSpec((ROW_TILE, d), lambda i: (lat_tile(i), 0))]
        args += list(x)
    else:
        in_specs.append(row_spec(d))
        args.append(x)
    in_specs.append(pl.BlockSpec((None, N_MOD, d),
                                 lambda i: (_group_of_tile(i + skip, tiles_per_batch), 0, 0)))
    args.append(mods)
    ctx_mix = False
    if mix is not None:
        ya, yb, yc, yd_lat, yd_ctx, wo_parts = mix
        assert ROW_TILE == 2 * CHUNK
        n_b = ctx_rows // CHUNK
        cpb = tiles_per_batch * (ROW_TILE // CHUNK)
        pb_half = lambda y, half: pl.BlockSpec(
            (CHUNK, y.shape[1]), lambda i: (_pb_chunk(2 * (i + skip) + half, n_b, cpb), 0))
        in_specs += [row_spec(ya.shape[1]), pb_half(yb, 0), pb_half(yb, 1), pb_half(yc, 0), pb_half(yc, 1),
                     pl.BlockSpec((ROW_TILE, yd_lat.shape[1]), lambda i: (lat_tile(i), 0))]
        args += [ya, yb, yb, yc, yc, yd_lat]
        ctx_mix = yd_ctx is not None
        if ctx_mix:
            in_specs.append(pl.BlockSpec((ROW_TILE, yd_ctx.shape[1]), lambda i: (0, 0)))
            args.append(yd_ctx)
        in_specs += [weight_spec(w) for w in wo_parts]
        args += list(wo_parts)
    in_specs += [weight_spec(w13), weight_spec(w2)]
    args += [w13, w2]
    if final:
        in_specs.append(pl.BlockSpec((1, d), lambda i: (0, 0)))
        args.append(final_gain.reshape(1, d))
    kern = functools.partial(_ffn_kernel, row0=row0, d_ff=d_ff, chunks=chunks, split_input=split_input,
                             mix=mix is not None, ctx_mix=ctx_mix, final=final)
    out_rows = n - skip * ROW_TILE
    return pl.pallas_call(
        kern,
        out_shape=jax.ShapeDtypeStruct((out_rows, d), F32),
        grid=(out_rows // ROW_TILE,),
        in_specs=in_specs,
        out_specs=pl.BlockSpec((ROW_TILE, d), lambda i: (i, 0)),
        compiler_params=_cparams(("parallel",)),
        name="ffn_mix" if mix is not None else "ffn",
    )(*args)


def _win_kernel(x_ref, mod_ref, wa_ref, wb_ref, wc_ref, wd_ref, cos_ref, sin_ref, qn_ref, kvn_ref,
                wqm_ref, wqr_ref, wk_ref, wv_ref, oa_ref, ob_ref, oc_ref, q_ref, k_ref, v_ref):
    xm = _modulate(x_ref[...], mod_ref[3:4, :], mod_ref[4:5, :]).astype(BF16)
    oa_ref[...] = jnp.dot(xm, wa_ref[...], preferred_element_type=F32)
    ob_ref[...] = jnp.dot(xm, wb_ref[...], preferred_element_type=F32)
    oc_ref[...] = jnp.dot(xm, wc_ref[...], preferred_element_type=F32)
    u_d = jnp.dot(xm, wd_ref[...], preferred_element_type=F32)
    _mla_project(u_d, cos_ref, sin_ref, qn_ref, kvn_ref, wqm_ref, wqr_ref, wk_ref, wv_ref, q_ref, k_ref, v_ref)


def _win_call(x, mods, wa, wb, wc, wd, cos_tab, sin_tab, mla_params, *, layer, n_batch, cpb):
    n, d = x.shape
    hw = MLA_HEADS * LANE
    widths = (wa.shape[2], wb.shape[2], wc.shape[2])
    group = lambda c: jnp.where(c < n_batch, 0, 1 + jnp.maximum(c - n_batch, 0) // cpb)
    flat = lambda c: (c, 0)
    per_batch = lambda c: (_pb_chunk(c, n_batch, cpb), 0)
    kv_cpb = cpb + 1

    def kv_chunk(c):
        lat = jnp.maximum(c - n_batch, 0)
        bb = lat // cpb
        return (jnp.where(c < n_batch, c * kv_cpb + cpb, bb * kv_cpb + (lat - bb * cpb)), 0)

    def tab_chunk(c):
        return (jnp.where(c < n_batch, 0, 1 + lax.rem(jnp.maximum(c - n_batch, 0), cpb)), 0)

    out_maps = (flat, per_batch, per_batch, flat, kv_chunk, kv_chunk)
    out_widths = widths + (hw, hw, hw)
    out_dtypes = (F32, F32, F32, BF16, BF16, BF16)
    return pl.pallas_call(
        _win_kernel,
        out_shape=tuple(jax.ShapeDtypeStruct((n, w), dt) for w, dt in zip(out_widths, out_dtypes)),
        grid=(n // CHUNK,),
        in_specs=[
            pl.BlockSpec((CHUNK, d), flat),
            pl.BlockSpec((None, N_MOD, d), lambda c: (group(c), 0, 0)),
        ] + [_param_spec(w, (layer,)) for w in (wa, wb, wc, wd)]
          + [pl.BlockSpec((CHUNK, LANE), tab_chunk), pl.BlockSpec((CHUNK, LANE), tab_chunk)]
          + [_param_spec(arr, (layer,)) for arr in mla_params],
        out_specs=tuple(pl.BlockSpec((CHUNK, w), m) for w, m in zip(out_widths, out_maps)),
        compiler_params=_cparams(("parallel",)),
        name="w_in",
    )(x, mods, wa, wb, wc, wd, cos_tab, sin_tab, *mla_params)


def _seq_flags(c, n_batch, cpb):
    j = lax.rem(jnp.maximum(c - n_batch, 0), cpb)
    is_ctx = c < n_batch
    first = jnp.logical_or(is_ctx, j == 0)
    last = jnp.logical_or(is_ctx, j == cpb - 1)
    return first, last


def _pb_chunk(c, n_batch, cpb):
    lat = jnp.maximum(c - n_batch, 0)
    b = lat // cpb
    return jnp.where(c < n_batch, c * (cpb + 1), b * (cpb + 1) + 1 + (lat - b * cpb))


def _pb_scan_chunk(i, cpb, reverse):
    if reverse:
        return jnp.where(i == 0, 0, cpb + 1 - i)
    return i


def _conv_kernel(cur_ref, prev_ref, next_ref, w_ref, b_ref, g_ref, beta_ref, o_ref, zbuf, zrot,
                 *, n_batch, cpb):
    c = pl.program_id(0)
    first, last = _seq_flags(c, n_batch, cpb)

    def glu(u):
        return u[:, :GROUP_W] * _sigmoid(u[:, GROUP_W:])

    zp = glu(prev_ref[...])
    zn = glu(next_ref[...])
    zbuf[0:CONV_HALO, :] = jnp.where(first, 0.0, zp)
    zbuf[CONV_HALO:CONV_HALO + CHUNK, :] = glu(cur_ref[...])
    zbuf[CONV_HALO + CHUNK:2 * CONV_HALO + CHUNK, :] = jnp.where(last, 0.0, zn)

    pad = CONV_K // 2
    span = CHUNK + 2 * CONV_HALO - SUBLANE
    acc = jnp.zeros((CHUNK, GROUP_W), F32) + b_ref[...]
    for phase in range(SUBLANE):
        if phase:
            zrot[phase - 1] = zbuf[phase:phase + span, :]
        for j in range(CONV_K):
            start = CONV_HALO - pad + j
            if start % SUBLANE == phase:
                lo = start - phase
                win = zrot[phase - 1, lo:lo + CHUNK, :] if phase else zbuf[lo:lo + CHUNK, :]
                acc = acc + w_ref[j:j + 1, :] * win
    mu = jnp.mean(acc, axis=-1, keepdims=True)
    xc = acc - mu
    var = jnp.mean(xc * xc, axis=-1, keepdims=True)
    y = xc * lax.rsqrt(var + LN_EPS) * g_ref[...] + beta_ref[...]
    o_ref[...] = (y * _sigmoid(y)).astype(o_ref.dtype)


def _conv_call(u_a, dw_w, dw_b, ln_g, ln_b, *, layer, n_batch, cpb):
    n = u_a.shape[0]
    n_chunks = n // CHUNK
    hpc = CHUNK // CONV_HALO
    n_halo = n // CONV_HALO
    kern = functools.partial(_conv_kernel, n_batch=n_batch, cpb=cpb)
    return pl.pallas_call(
        kern,
        out_shape=jax.ShapeDtypeStruct((n, GROUP_W), BF16),
        grid=(n_chunks,),
        in_specs=[
            pl.BlockSpec((CHUNK, 2 * GROUP_W), lambda c: (c, 0)),
            pl.BlockSpec((CONV_HALO, 2 * GROUP_W), lambda c: (jnp.maximum(c * hpc - 1, 0), 0)),
            pl.BlockSpec((CONV_HALO, 2 * GROUP_W),
                         lambda c: (jnp.minimum((c + 1) * hpc, n_halo - 1), 0)),
        ] + [_param_spec(arr, (layer,)) for arr in (dw_w, dw_b, ln_g, ln_b)],
        out_specs=pl.BlockSpec((CHUNK, GROUP_W), lambda c: (c, 0)),
        scratch_shapes=[pltpu.VMEM((CHUNK + 2 * CONV_HALO, GROUP_W), F32),
                        pltpu.VMEM((SUBLANE - 1, CHUNK + 2 * CONV_HALO - SUBLANE, GROUP_W), F32)],
        compiler_params=_cparams(("parallel",)),
        name="conformer_conv",
    )(u_a, u_a, u_a, dw_w, dw_b, ln_g, ln_b)


def _lru_kernel(*refs, n_batch, cpb, reverse):
    if reverse:
        cur_ref, prev_ref, next_ref, cw_ref, cb_ref, wg_ref, bg_ref, lam_ref, o_ref, xbuf, carry = refs
        hrev_ref = None
    else:
        (cur_ref, prev_ref, next_ref, cw_ref, cb_ref, wg_ref, bg_ref, lam_ref, hrev_ref,
         o_ref, xbuf, carry) = refs
    i = pl.program_id(0)
    jc = _pb_scan_chunk(i, cpb, reverse)
    first = jnp.logical_or(i == 0, jc == 1)
    last = jnp.logical_or(i == 0, jc == cpb)
    rows_all = n_batch * CHUNK

    @pl.when(i == 0)
    def _():
        carry[...] = jnp.zeros_like(carry)

    u = cur_ref[...]
    gb = u[:, :, GROUP_W:].reshape(rows_all, GROUP_W)
    xbuf[:, 0:SUBLANE, :] = jnp.where(first, 0.0, prev_ref[:, :, :GROUP_W])
    xbuf[:, SUBLANE:SUBLANE + CHUNK, :] = u[:, :, :GROUP_W]
    xbuf[:, SUBLANE + CHUNK:2 * SUBLANE + CHUNK, :] = jnp.where(last, 0.0, next_ref[:, :, :GROUP_W])
    pad_l = LRU_CONV_K // 2
    xv = jnp.zeros((rows_all, GROUP_W), F32) + cb_ref[...]
    for j in range(LRU_CONV_K):
        start = SUBLANE - pad_l + j
        xv = xv + cw_ref[j:j + 1, :] * xbuf[:, start:start + CHUNK, :].reshape(rows_all, GROUP_W)

    z = _mm(xv, wg_ref[...]) + bg_ref[...]
    r = _sigmoid(z[:, :GROUP_W])
    ig = _sigmoid(z[:, GROUP_W:])
    log_a = (-LRU_C) * r * _softplus(-lam_ref[...])
    a = jnp.exp(log_a)
    t = jnp.tanh(log_a)
    om = -2.0 * t / (1.0 - t)
    bb = jnp.where(om > 0.0, om * lax.rsqrt(om), 0.0) * (ig * xv)

    groups = CHUNK // SUBLANE
    g3 = (rows_all // SUBLANE, SUBLANE, GROUP_W)
    a = a.reshape(g3)
    bb = bb.reshape(g3)
    row = lax.broadcasted_iota(jnp.int32, g3, 1)
    s = 1
    while s < SUBLANE:
        if reverse:
            a_sh = pltpu.roll(a, SUBLANE - s, axis=1)
            b_sh = pltpu.roll(bb, SUBLANE - s, axis=1)
            valid = row < SUBLANE - s
        else:
            a_sh = pltpu.roll(a, s, axis=1)
            b_sh = pltpu.roll(bb, s, axis=1)
            valid = row >= s
        bb = jnp.where(valid, a * b_sh + bb, bb)
        a = jnp.where(valid, a * a_sh, a)
        s *= 2
    a = a.reshape(rows_all, GROUP_W)
    bb = bb.reshape(rows_all, GROUP_W)
    state = [carry[bi] for bi in range(n_batch)]
    h_parts = {}
    edge_row = 0 if reverse else SUBLANE - 1
    for g in (range(groups - 1, -1, -1) if reverse else range(groups)):
        for bi in range(n_batch):
            r0 = bi * CHUNK + g * SUBLANE
            hg = bb[r0:r0 + SUBLANE, :] + a[r0:r0 + SUBLANE, :] * state[bi]
            h_parts[(bi, g)] = hg
            state[bi] = jnp.broadcast_to(hg[edge_row:edge_row + 1, :], (SUBLANE, GROUP_W))
    for bi in range(n_batch):
        carry[bi] = state[bi]
    h = jnp.concatenate([h_parts[(bi, g)] for bi in range(n_batch) for g in range(groups)], axis=0)

    if reverse:
        o_ref[...] = h.reshape(n_batch, CHUNK, GROUP_W)
    else:
        out = (h + hrev_ref[...].reshape(rows_all, GROUP_W)) * _gelu_tanh(gb)
        o_ref[...] = out.astype(o_ref.dtype).reshape(n_batch, CHUNK, GROUP_W)


def _lru_call(u_b, conv_w, conv_b, w_gate, b_gate, lam, h_rev, *, layer, n_batch, cpb, reverse):
    n, wb = u_b.shape
    bpc = CHUNK // SUBLANE
    cps = cpb + 1
    assert n == n_batch * cps * CHUNK
    chunk = lambda i: _pb_scan_chunk(i, cpb, reverse)
    u4 = u_b.reshape(n_batch, cps, CHUNK, wb)
    u8 = u_b.reshape(n_batch, cps * bpc, SUBLANE, wb)
    d = 1 if reverse else 0
    seq_spec = pl.BlockSpec((n_batch, None, CHUNK, GROUP_W), lambda i: (0, chunk(i), 0, 0))
    in_specs = [
        pl.BlockSpec((n_batch, None, CHUNK, wb), lambda i: (0, chunk(i), 0, 0)),
        pl.BlockSpec((n_batch, None, SUBLANE, wb), lambda i: (0, jnp.maximum(chunk(i) * bpc - 1, 0), 0, 0)),
        pl.BlockSpec((n_batch, None, SUBLANE, wb),
                     lambda i: (0, jnp.minimum((chunk(i) + 1) * bpc, cps * bpc - 1), 0, 0)),
        _param_spec(conv_w, (layer,)), _param_spec(conv_b, (layer,)), _param_spec(w_gate, (layer, d)),
        _param_spec(b_gate, (layer, d)), _param_spec(lam, (layer, d)),
    ]
    args = [u4, u8, u8, conv_w, conv_b, w_gate, b_gate, lam]
    if not reverse:
        in_specs.append(seq_spec)
        args.append(h_rev)
    kern = functools.partial(_lru_kernel, n_batch=n_batch, cpb=cpb, reverse=reverse)
    out = pl.pallas_call(
        kern,
        out_shape=jax.ShapeDtypeStruct((n_batch, cps, CHUNK, GROUP_W), F32 if reverse else BF16),
        grid=(cps,),
        in_specs=in_specs,
        out_specs=seq_spec,
        scratch_shapes=[pltpu.VMEM((n_batch, CHUNK + 2 * SUBLANE, GROUP_W), F32),
                        pltpu.VMEM((n_batch, SUBLANE, GROUP_W), F32)],
        compiler_params=_cparams(("arbitrary",)),
        name="rglru_rev" if reverse else "rglru_fwd",
    )(*args)
    return out if reverse else out.reshape(n, GROUP_W)


def _rwkv_kernel(*refs, n_batch, cpb, reverse):
    if reverse:
        (cur_ref, prev_ref, next_ref, mup_ref, mun_ref, w0_ref, wup_ref, a0_ref, aup_ref, gup_ref,
         kk_ref, ka_ref, rk_ref, gng_ref, gnb_ref, hones_ref,
         y_ref, bonus_ref, ubuf, st_scr) = refs
        yrev_ref = brev_ref = None
    else:
        (cur_ref, prev_ref, next_ref, mup_ref, mun_ref, w0_ref, wup_ref, a0_ref, aup_ref, gup_ref,
         kk_ref, ka_ref, rk_ref, gng_ref, gnb_ref, hones_ref, yrev_ref, brev_ref,
         y_ref, ubuf, st_scr) = refs
    i = pl.program_id(0)
    j = _pb_scan_chunk(i, cpb, reverse)
    first = jnp.logical_or(i == 0, j == 1)
    last = jnp.logical_or(i == 0, j == cpb)
    g4 = RWKV_HEADS * RWKV_HEAD
    assert g4 == GROUP_W and RWKV_HEADS * SUB == GROUP_W
    rows_all = n_batch * CHUNK
    wc = cur_ref.shape[-1]

    @pl.when(i == 0)
    def _():
        st_scr[...] = jnp.zeros_like(st_scr)

    ubuf[:, 0:SUBLANE, :] = jnp.where(first, 0.0, prev_ref[...])
    ubuf[:, SUBLANE:SUBLANE + CHUNK, :] = cur_ref[...]
    ubuf[:, SUBLANE + CHUNK:2 * SUBLANE + CHUNK, :] = jnp.where(last, 0.0, next_ref[...])
    u = cur_ref[...].reshape(rows_all, wc)
    up = ubuf[:, SUBLANE - 1:SUBLANE - 1 + CHUNK, :].reshape(rows_all, wc)
    un = ubuf[:, SUBLANE + 1:SUBLANE + 1 + CHUNK, :].reshape(rows_all, wc)
    vs = u + mup_ref[...] * (up - u) + mun_ref[...] * (un - u)

    hones = hones_ref[...]
    r = vs[:, 0:GROUP_W]
    k = vs[:, GROUP_W:2 * GROUP_W]
    val = vs[:, 2 * GROUP_W:3 * GROUP_W]
    lora_in = vs[:, 3 * GROUP_W:3 * GROUP_W + LANE]
    gate_in = vs[:, 3 * GROUP_W + LANE:]
    w_lin = _mm(jnp.tanh(lora_in), wup_ref[...])
    a_lin = _mm(lora_in, aup_ref[...])
    kq = k * kk_ref[...]
    ss = _mm_exact_rhs(kq * kq, hones)
    kk = kq * lax.rsqrt(jnp.maximum(ss, 1e-24))
    lw = -jnp.exp(-_softplus(-(w0_ref[...] + w_lin)) - 0.5)
    a = _sigmoid(a0_ref[...] + a_lin)
    kd = k * (1.0 + (a - 1.0) * ka_ref[...])
    bonus = _mm_exact_rhs(r * kd * rk_ref[...], hones) * val

    bvec = kk * a

    ri = lax.broadcasted_iota(jnp.int32, (g4, g4), 0)
    ci = lax.broadcasted_iota(jnp.int32, (g4, g4), 1)
    same = (ri // SUB) == (ci // SUB)
    rt = lax.rem(ri, SUB)
    ct = lax.rem(ci, SUB)
    if reverse:
        strict = jnp.logical_and(same, ct > rt)
        incl = jnp.logical_and(same, ct >= rt)
    else:
        strict = jnp.logical_and(same, ct < rt)
        incl = jnp.logical_and(same, ct <= rt)
    eye = ri == ci
    ti = lax.broadcasted_iota(jnp.int32, (SUB, SUB), 0)
    tj = lax.broadcasted_iota(jnp.int32, (SUB, SUB), 1)
    tri = jnp.where((tj >= ti) if reverse else (tj <= ti), 1.0, 0.0).astype(BF16)
    lane_head = lax.broadcasted_iota(jnp.int32, (1, g4), 1) // RWKV_HEAD

    def stack(x):
        return jnp.concatenate([jnp.where(lane_head == h, x, 0.0) for h in range(RWKV_HEADS)], axis=0)

    def unstack(x):
        out = x[0:SUB, :]
        for h in range(1, RWKV_HEADS):
            out = out + x[h * SUB:(h + 1) * SUB, :]
        return out

    per_batch = CHUNK // SUB
    n_sub = n_batch * per_batch
    subs = range(n_sub)
    rows = [slice(sc * SUB, (sc + 1) * SUB) for sc in subs]
    each = lambda f, *lists: [f(*xs) for xs in zip(*lists)]
    lw_s = [lw[rw] for rw in rows]
    kk_s = [kk[rw] for rw in rows]
    b_s = [bvec[rw] for rw in rows]
    kd_s = [kd[rw] for rw in rows]
    r_s = [r[rw] for rw in rows]
    v_s = [val[rw] for rw in rows]

    cum = each(lambda x: _mm_exact_lhs(tri, x), lw_s)
    edge = each(lambda x: x[0:1, :] if reverse else x[SUB - 1:SUB, :], cum)
    a_st = each(lambda cu, l, x: stack(jnp.exp(cu - l) * x), cum, lw_s, kk_s)
    b_st = each(lambda cu, x: stack(x * jnp.exp(-cu)), cum, b_s)
    k_st = each(lambda cu, x: stack(x * jnp.exp(-cu)), cum, kd_s)
    r_st = each(lambda cu, x: stack(x * jnp.exp(cu)), cum, r_s)
    v_bf = each(lambda x: stack(x).astype(BF16), v_s)
    e_out = each(lambda ed, cu: jnp.exp(ed - cu), edge, cum)
    beta_w = each(lambda x, e: x * e, b_s, e_out)
    kappa_w = each(lambda x, e: x * e, kd_s, e_out)
    w_all = each(jnp.exp, edge)

    prod = each(lambda a_, r_, b_, k_: _mm_nt(jnp.concatenate([a_, r_], axis=0),
                                              jnp.concatenate([b_, k_], axis=0)),
                a_st, r_st, b_st, k_st)
    l_ak = each(lambda p_: jnp.where(strict, p_[0:g4, g4:], 0.0), prod)
    m_rb = each(lambda p_: jnp.where(incl, p_[g4:, 0:g4], 0.0).astype(BF16), prod)
    m_rk = each(lambda p_: jnp.where(incl, p_[g4:, g4:], 0.0), prod)

    l_ab = each(lambda p_: jnp.where(strict, p_[0:g4, 0:g4], 0.0), prod)
    q = each(lambda l_: -l_, l_ab)
    tm = q
    for _ in range(5):
        q = each(lambda q_: _mm(q_, q_), q)
        tm = each(lambda t_, q_: t_ + q_ + _mm(t_, q_), tm, q)

    def refine(t_, l_):
        res = -(t_ + l_) - _mm3(l_, t_)
        return t_ + res + _mm(t_, res)

    tm = each(refine, tm, l_ab)

    lakv = each(_mm, l_ak, v_bf)
    rhs = each(lambda a_, x: jnp.concatenate([a_, x], axis=1), a_st, lakv)
    sol = each(lambda t_, x: x + _mm(t_, x), tm, rhs)
    corr = each(_mm, m_rb, sol)
    ra_st = each(lambda r_, c_: r_ - c_[:, 0:g4], r_st, corr)
    y0_st = each(lambda m_, v_, c_: _mm(m_, v_) - c_[:, g4:], m_rk, v_bf, corr)
    ta = each(lambda s_: unstack(s_[:, 0:g4]), sol)
    z0 = each(lambda s_: unstack(s_[:, g4:]), sol)
    gt = each(lambda w_, bw, ta_: jnp.where(eye, w_, 0.0) - jnp.where(same, _mm_tn(bw, ta_), 0.0),
              w_all, beta_w, ta)
    ht = each(lambda kw, v_, bw, z_: jnp.where(same, _mm_tn(kw, v_) - _mm_tn(bw, z_), 0.0),
              kappa_w, v_s, beta_w, z0)

    order = list(range(per_batch - 1, -1, -1)) if reverse else list(range(per_batch))
    st = [st_scr[bi] for bi in range(n_batch)]
    y_parts = {}
    for t in order:
        for bi in range(n_batch):
            sc = bi * per_batch + t
            y_parts[sc] = unstack(_mm(ra_st[sc], st[bi]) + y0_st[sc])
            st[bi] = _mm3(gt[sc], st[bi]) + ht[sc]
    for bi in range(n_batch):
        st_scr[bi] = st[bi]
    y_all = jnp.concatenate([y_parts[sc] for sc in subs], axis=0)

    if reverse:
        y_ref[...] = y_all.reshape(n_batch, CHUNK, GROUP_W)
        bonus_ref[...] = bonus.reshape(n_batch, CHUNK, GROUP_W)
    else:
        yy = y_all + yrev_ref[...].reshape(rows_all, GROUP_W)
        inv_n = 1.0 / RWKV_HEAD
        mu = _mm_exact_rhs(yy, hones) * inv_n
        yc = yy - mu
        var = _mm_exact_rhs(yc * yc, hones) * inv_n
        o = (yc * lax.rsqrt(var + GN_EPS) * gng_ref[...] + gnb_ref[...] + bonus
             + brev_ref[...].reshape(rows_all, GROUP_W))
        gate = _mm(_sigmoid(gate_in), gup_ref[...])
        y_ref[...] = (o * gate).astype(y_ref.dtype).reshape(n_batch, CHUNK, GROUP_W)


def _param_spec(arr, lead):
    shape = (None,) * len(lead) + arr.shape[len(lead):]
    index = tuple(lead) + (0,) * (arr.ndim - len(lead))
    return pl.BlockSpec(shape, lambda *_: index)


def _rwkv_params(mu_prev, mu_next, w0, w_up, a0, a_up, g_up, k_k, k_a, r_k, gn_g, gn_b):
    depth = mu_prev.shape[0]
    row = lambda v: v.reshape(v.shape[:-1] + (1, v.shape[-1]))
    zeros64 = jnp.zeros(w_up.shape[:2] + (RWKV_HEAD, GROUP_W), F32)
    hid = jnp.arange(GROUP_W) // RWKV_HEAD
    return dict(
        mu_prev=row(mu_prev), mu_next=row(mu_next), w0=row(w0), a0=row(a0),
        w_up=jnp.concatenate([w_up, zeros64], axis=2).astype(BF16),
        a_up=jnp.concatenate([zeros64, a_up], axis=2).astype(BF16),
        g_up=g_up.astype(BF16), k_k=row(k_k), k_a=row(k_a), r_k=row(r_k.reshape(depth, GROUP_W)),
        gn_g=row(gn_g), gn_b=row(gn_b),
        hones=(hid[:, None] == hid[None, :]).astype(BF16))


def _rwkv_call(u_c, p, y_rev, bonus_rev, *, layer, n_batch, cpb, reverse):
    n, wc = u_c.shape
    bpc = CHUNK // SUBLANE
    cps = cpb + 1
    assert n == n_batch * cps * CHUNK
    chunk = lambda i: _pb_scan_chunk(i, cpb, reverse)
    u4 = u_c.reshape(n_batch, cps, CHUNK, wc)
    u8 = u_c.reshape(n_batch, cps * bpc, SUBLANE, wc)
    d = 1 if reverse else 0
    small = [(p['mu_prev'], (layer,)), (p['mu_next'], (layer,)), (p['w0'], (layer, d)), (p['w_up'], (layer, d)),
             (p['a0'], (layer, d)), (p['a_up'], (layer, d)), (p['g_up'], (layer,)), (p['k_k'], (layer,)),
             (p['k_a'], (layer,)), (p['r_k'], (layer,)), (p['gn_g'], (layer,)), (p['gn_b'], (layer,)),
             (p['hones'], ())]
    in_specs = [
        pl.BlockSpec((n_batch, None, CHUNK, wc), lambda i: (0, chunk(i), 0, 0)),
        pl.BlockSpec((n_batch, None, SUBLANE, wc), lambda i: (0, jnp.maximum(chunk(i) * bpc - 1, 0), 0, 0)),
        pl.BlockSpec((n_batch, None, SUBLANE, wc),
                     lambda i: (0, jnp.minimum((chunk(i) + 1) * bpc, cps * bpc - 1), 0, 0)),
    ] + [_param_spec(arr, lead) for arr, lead in small]
    args = [u4, u8, u8] + [arr for arr, _ in small]
    seq_spec = pl.BlockSpec((n_batch, None, CHUNK, GROUP_W), lambda i: (0, chunk(i), 0, 0))
    seq_shape = (n_batch, cps, CHUNK, GROUP_W)
    scratch = [pltpu.VMEM((n_batch, CHUNK + 2 * SUBLANE, wc), F32),
               pltpu.VMEM((n_batch, GROUP_W, GROUP_W), F32)]
    if reverse:
        out_shape = (jax.ShapeDtypeStruct(seq_shape, F32), jax.ShapeDtypeStruct(seq_shape, F32))
        out_specs = (seq_spec, seq_spec)
    else:
        in_specs += [seq_spec, seq_spec]
        args += [y_rev, bonus_rev]
        out_shape = jax.ShapeDtypeStruct(seq_shape, BF16)
        out_specs = seq_spec
    kern = functools.partial(_rwkv_kernel, n_batch=n_batch, cpb=cpb, reverse=reverse)
    out = pl.pallas_call(
        kern,
        out_shape=out_shape,
        grid=(cps,),
        in_specs=in_specs,
        out_specs=out_specs,
        scratch_shapes=scratch,
        compiler_params=_cparams(("arbitrary",)),
        name="rwkv7_rev" if reverse else "rwkv7_fwd",
    )(*args)
    return out if reverse else out.reshape(n, GROUP_W)


def _mla_project(u, cos_ref, sin_ref, qn_ref, kvn_ref, wqm_ref, wqr_ref, wk_ref, wv_ref, q_ref, k_ref, v_ref):
    cq = u[:, :Q_LORA]
    ckv = u[:, Q_LORA:Q_LORA + KV_LORA]
    blk_r = u[:, Q_LORA + KV_LORA:Q_LORA + KV_LORA + LANE]
    blk_rr = u[:, Q_LORA + KV_LORA + LANE:]
    cqn = (cq * lax.rsqrt(jnp.mean(cq * cq, axis=-1, keepdims=True) + EPS) * qn_ref[...]).astype(BF16)
    ckvn = (ckv * lax.rsqrt(jnp.mean(ckv * ckv, axis=-1, keepdims=True) + EPS) * kvn_ref[...]).astype(BF16)
    cos_t = cos_ref[...]
    sin_t = sin_ref[...]
    cos4 = jnp.concatenate([cos_t] * MLA_HEADS, axis=1)
    sin4 = jnp.concatenate([sin_t] * MLA_HEADS, axis=1)
    qm = jnp.dot(cqn, wqm_ref[...], preferred_element_type=F32)
    qr = jnp.dot(cqn, wqr_ref[...], preferred_element_type=F32)
    q_ref[...] = ((qm * cos4 + qr * sin4) * (SM_SCALE * LOG2_E)).astype(q_ref.dtype)
    kr = blk_r * cos_t + blk_rr * sin_t
    km = jnp.dot(ckvn, wk_ref[...], preferred_element_type=F32)
    k_ref[...] = (km + jnp.concatenate([kr] * MLA_HEADS, axis=1)).astype(k_ref.dtype)
    lane = lax.broadcasted_iota(jnp.int32, (1, MLA_HEADS * LANE), 1)
    ones_col = jnp.where(lax.rem(lane, LANE) == V_HEAD, 1.0, 0.0)
    v_ref[...] = (jnp.dot(ckvn, wv_ref[...], preferred_element_type=F32) + ones_col).astype(v_ref.dtype)


def _attn_kernel(q_ref, k_ref, v_ref, o_ref, m_scr, acc_scr, *, n_kv, unit_keys):
    kj = pl.program_id(2)

    @pl.when(kj == 0)
    def _():
        m_scr[...] = jnp.full_like(m_scr, -jnp.inf)
        acc_scr[...] = jnp.zeros_like(acc_scr)

    tk = k_ref.shape[0]
    bounds = list(range(0, tk, unit_keys)) + [tk]
    units = [(h, lo, hi) for h in range(MLA_HEADS) for lo, hi in zip(bounds[:-1], bounds[1:])]

    def scores(u):
        h, lo, hi = u
        sl = slice(h * LANE, (h + 1) * LANE)
        return lax.dot_general(q_ref[:, sl], k_ref[lo:hi, sl], (((1,), (1,)), ((), ())),
                               preferred_element_type=F32)

    def softmax_step(u, s):
        h = u[0]
        m_prev = m_scr[h]
        m_new = jnp.maximum(m_prev, jnp.max(s, axis=-1, keepdims=True))
        m_scr[h] = m_new
        return jnp.exp2(s - m_new).astype(BF16), jnp.exp2(m_prev - m_new)

    def weighted_values(u, p, alpha):
        h, lo, hi = u
        sl = slice(h * LANE, (h + 1) * LANE)
        acc_scr[h] = alpha * acc_scr[h] + jnp.dot(p, v_ref[lo:hi, sl], preferred_element_type=F32)

    s_next = scores(units[0])
    pending = None
    for idx, u in enumerate(units):
        s_cur = s_next
        if idx + 1 < len(units):
            s_next = scores(units[idx + 1])
        p, alpha = softmax_step(u, s_cur)
        if pending is not None:
            weighted_values(*pending)
        pending = (u, p, alpha)
    weighted_values(*pending)

    @pl.when(kj == n_kv - 1)
    def _():
        for h in range(MLA_HEADS):
            sl = slice(h * LANE, (h + 1) * LANE)
            acc = acc_scr[h]
            o_ref[:, sl] = (acc / acc[:, V_HEAD:V_HEAD + 1]).astype(o_ref.dtype)


def _attn_call(q, k, v, *, out_rows, n_batch, tq, tk, n_q, n_kv, q_block, kv_block, o_block, name):
    hw = q.shape[1]
    unit = min(ATTN_UNIT_KEYS, tk)
    kern = functools.partial(_attn_kernel, n_kv=n_kv, unit_keys=unit)
    return pl.pallas_call(
        kern,
        out_shape=jax.ShapeDtypeStruct((out_rows, hw), BF16),
        grid=(n_batch, n_q, n_kv),
        in_specs=[
            pl.BlockSpec((tq, hw), lambda b, qi, kj: (q_block(b, qi), 0)),
            pl.BlockSpec((tk, hw), lambda b, qi, kj: (kv_block(b, kj), 0)),
            pl.BlockSpec((tk, hw), lambda b, qi, kj: (kv_block(b, kj), 0)),
        ],
        out_specs=pl.BlockSpec((tq, hw), lambda b, qi, kj: (o_block(b, qi), 0)),
        scratch_shapes=[pltpu.VMEM((MLA_HEADS, tq, 1), F32), pltpu.VMEM((MLA_HEADS, tq, LANE), F32)],
        compiler_params=_cparams(("parallel", "parallel", "arbitrary")),
        name=name,
    )(q, k, v)


def _kv_tile(n_keys):
    best = LANE
    t = LANE
    while t <= ATTN_MAX_KEYS:
        if n_keys % t == 0:
            best = t
        t += LANE
    return best


def _rot_cols(w):
    q = QK_ROPE // 4
    return jnp.concatenate([-w[..., q:2 * q], w[..., 0:q], -w[..., 3 * q:4 * q], w[..., 2 * q:3 * q]],
                           axis=-1)


def _block_diag(w):
    nb, n = w.shape[-3], w.shape[-2]
    eye = jnp.eye(nb, dtype=w.dtype)
    out = eye[:, None, :, None] * w[..., :, :, None, :]
    return out.reshape(w.shape[:-3] + (nb * n, nb * n))


def _rope_tables(t_len, ctx_len):
    t = jnp.arange(t_len, dtype=jnp.int32)
    rows = (t // GRID_W).astype(F32)
    cols = (t % GRID_W).astype(F32)
    n_freq = QK_ROPE // 4
    inv_freq = ROPE_BASE ** (-jnp.arange(n_freq, dtype=F32) / n_freq)
    ang = jnp.stack([rows[:, None] * inv_freq, cols[:, None] * inv_freq], axis=1)
    ang = jnp.concatenate([ang, ang], axis=-1).reshape(t_len, QK_ROPE)
    cos = jnp.concatenate([jnp.ones((ctx_len, QK_ROPE), F32), jnp.cos(ang)], axis=0)
    sin = jnp.concatenate([jnp.zeros((ctx_len, QK_ROPE), F32), jnp.sin(ang)], axis=0)
    n = t_len + ctx_len
    pad = LANE - QK_NOPE - QK_ROPE
    cos_tab = jnp.concatenate([jnp.ones((n, QK_NOPE), F32), cos, jnp.zeros((n, pad), F32)], axis=1)
    sin_tab = jnp.concatenate([jnp.zeros((n, QK_NOPE), F32), sin, jnp.zeros((n, pad), F32)], axis=1)
    return cos_tab, sin_tab


def kernel(x, c, ctx, c_ctx, ada_w, ada_b, ffn1_w13, ffn1_w2, ffn2_w13, ffn2_w2, w_in, w_out,
           cv_dw_w, cv_dw_b, cv_ln_g, cv_ln_b,
           lru_conv_w, lru_conv_b, lru_wa, lru_ba, lru_wx, lru_bx, lru_lambda,
           rwkv_mu_prev, rwkv_mu_next, rwkv_w0, rwkv_w_up, rwkv_a0, rwkv_a_up, rwkv_g_up,
           rwkv_k_k, rwkv_k_a, rwkv_r_k, rwkv_gn_g, rwkv_gn_b,
           mla_q_norm, mla_w_uq, mla_kv_norm, mla_w_ukv, final_norm):
    n_batch, t_len, d = x.shape
    ctx_len = ctx.shape[1]
    depth = ada_w.shape[0]
    assert ctx_len == CHUNK and n_batch * ctx_len == ROW_TILE and t_len % ROW_TILE == 0
    assert c.shape[0] + 1 <= SUBLANE
    ctx_rows = n_batch * ctx_len
    cpb = t_len // CHUNK
    tpb = t_len // ROW_TILE
    g = GROUP_W

    cvec = jnp.concatenate([c_ctx[None, :], c, jnp.zeros((SUBLANE - 1 - n_batch, d), F32)], axis=0)
    mods_all = _ada_call(cvec, ada_w, ada_b).reshape(depth, SUBLANE, N_MOD, d)

    cos_tab, sin_tab = _rope_tables(t_len, ctx_len)
    n_keys = t_len + ctx_len
    tk = _kv_tile(n_keys)
    tq = ROW_TILE
    zeros = jnp.zeros

    a_cols, b_cols, c_cols = 2 * g, 2 * g, 3 * g + 256
    w_a = w_in[:, :, :a_cols].astype(BF16)
    w_b = w_in[:, :, a_cols:a_cols + b_cols].astype(BF16)
    w_c = w_in[:, :, a_cols + b_cols:a_cols + b_cols + c_cols].astype(BF16)
    w_dq = w_in[:, :, a_cols + b_cols + c_cols:]
    w_kr = w_dq[:, :, Q_LORA + KV_LORA:]
    z64 = zeros((depth, d, QK_NOPE), F32)
    z32 = zeros((depth, d, LANE - QK_NOPE - QK_ROPE), F32)
    w_d = jnp.concatenate([w_dq[:, :, :Q_LORA + KV_LORA], z64, w_kr, z32, z64, _rot_cols(w_kr), z32],
                          axis=2).astype(BF16)

    hw = MLA_HEADS * LANE
    wq = mla_w_uq.reshape(depth, Q_LORA, MLA_HEADS, QK_NOPE + QK_ROPE)
    zq = zeros((depth, Q_LORA, MLA_HEADS, LANE - QK_NOPE - QK_ROPE), F32)
    wqm = jnp.concatenate([wq, zq], axis=-1).reshape(depth, Q_LORA, hw).astype(BF16)
    wqr = jnp.concatenate([zeros((depth, Q_LORA, MLA_HEADS, QK_NOPE), F32), _rot_cols(wq[..., QK_NOPE:]), zq],
                          axis=-1).reshape(depth, Q_LORA, hw).astype(BF16)
    wkv = mla_w_ukv.reshape(depth, KV_LORA, MLA_HEADS, QK_NOPE + V_HEAD)
    zk = zeros((depth, KV_LORA, MLA_HEADS, LANE - QK_NOPE), F32)
    wk = jnp.concatenate([wkv[..., :QK_NOPE], zk], axis=-1).reshape(depth, KV_LORA, hw).astype(BF16)
    wv = jnp.concatenate([wkv[..., QK_NOPE:], zk], axis=-1).reshape(depth, KV_LORA, hw).astype(BF16)

    wo_d = w_out[:, 3 * g:].reshape(depth, MLA_HEADS, V_HEAD, d)
    wo_d = jnp.concatenate([wo_d, zeros((depth, MLA_HEADS, LANE - V_HEAD, d), F32)], axis=2)
    wo_parts = [w_out[:, 0:g].astype(BF16), w_out[:, g:2 * g].astype(BF16), w_out[:, 2 * g:3 * g].astype(BF16),
                wo_d.reshape(depth, hw, d).astype(BF16)]

    row = lambda v: v.reshape(v.shape[:-1] + (1, v.shape[-1]))
    conv_rows = (row(cv_dw_b), row(cv_ln_g), row(cv_ln_b))
    lru_gate_w = jnp.concatenate([_block_diag(lru_wa), _block_diag(lru_wx)], axis=-1).astype(BF16)
    lru_gate_b = jnp.concatenate([lru_ba, lru_bx], axis=-1)
    lru_p = (row(lru_conv_b), lru_gate_w, row(lru_gate_b), row(lru_lambda))
    mla_p = (row(mla_q_norm), row(mla_kv_norm), wqm, wqr, wk, wv)
    ffn_w = [(ffn1_w13.astype(BF16), ffn1_w2.astype(BF16)), (ffn2_w13.astype(BF16), ffn2_w2.astype(BF16))]
    rwkv_p = _rwkv_params(rwkv_mu_prev, rwkv_mu_next, rwkv_w0, rwkv_w_up, rwkv_a0, rwkv_a_up, rwkv_g_up,
                          rwkv_k_k, rwkv_k_a, rwkv_r_k, rwkv_gn_g, rwkv_gn_b)

    ctx_flat = ctx.reshape(ctx_rows, d)
    x_flat = x.reshape(n_batch * t_len, d)
    xs = None
    out = None
    for l in range(depth):
        mods = mods_all[l]
        last = l == depth - 1

        xs = _ffn_call((ctx_flat, x_flat) if l == 0 else xs, mods, *ffn_w[0], layer=l, row0=0,
                       tiles_per_batch=tpb, ctx_rows=ctx_rows)

        u_a, u_b, u_c, q, k, v = _win_call(xs, mods, w_a, w_b, w_c, w_d, cos_tab, sin_tab, mla_p,
                                           layer=l, n_batch=n_batch, cpb=cpb)

        y_a = _conv_call(u_a, cv_dw_w, *conv_rows, layer=l, n_batch=n_batch, cpb=cpb)

        h_rev = _lru_call(u_b, lru_conv_w, *lru_p, None, layer=l, n_batch=n_batch, cpb=cpb, reverse=True)
        y_b = _lru_call(u_b, lru_conv_w, *lru_p, h_rev, layer=l, n_batch=n_batch, cpb=cpb, reverse=False)

        y_rev, bonus_rev = _rwkv_call(u_c, rwkv_p, None, None, layer=l, n_batch=n_batch, cpb=cpb, reverse=True)
        y_c = _rwkv_call(u_c, rwkv_p, y_rev, bonus_rev, layer=l, n_batch=n_batch, cpb=cpb, reverse=False)

        q_off = ctx_rows // tq
        o_lat = _attn_call(q, k, v, out_rows=n_batch * t_len, n_batch=n_batch, tq=tq, tk=tk,
                           n_q=t_len // tq, n_kv=n_keys // tk,
                           q_block=lambda b, qi: q_off + b * (t_len // tq) + qi,
                           kv_block=lambda b, kj: b * (n_keys // tk) + kj,
                           o_block=lambda b, qi: b * (t_len // tq) + qi, name="mla_attn")
        o_ctx = None if last else _attn_call(
            q, k, v, out_rows=ctx_rows, n_batch=n_batch, tq=CHUNK, tk=CHUNK, n_q=1, n_kv=1,
            q_block=lambda b, qi: b, kv_block=lambda b, kj: b * (cpb + 1) + cpb,
            o_block=lambda b, qi: b, name="mla_attn_ctx")

        mix = (y_a, y_b, y_c, o_lat, o_ctx, wo_parts)
        if last:
            out = _ffn_call(xs, mods, *ffn_w[1], layer=l, row0=6, tiles_per_batch=tpb, ctx_rows=ctx_rows,
                            mix=mix, final_gain=final_norm)
        else:
            xs = _ffn_call(xs, mods, *ffn_w[1], layer=l, row0=6, tiles_per_batch=tpb, ctx_rows=ctx_rows, mix=mix)

    return out.reshape(n_batch, t_len, d)
```

```python
import functools

import jax
import jax.numpy as jnp
from jax import lax
from jax.experimental import pallas as pl
from jax.experimental.pallas import tpu as pltpu

F32 = jnp.float32
BF16 = jnp.bfloat16

GRID_W = 64
N_MOD = 9
EPS = 1e-6
GROUP_W = 256
CONV_K = 31
LN_EPS = 1e-5
LRU_CONV_K = 4
LRU_C = 8.0
RWKV_HEAD = 64
RWKV_HEADS = 4
GN_EPS = 64e-5
QK_NOPE = 64
QK_ROPE = 32
V_HEAD = 64
MLA_HEADS = 4
Q_LORA = 256
KV_LORA = 128
ROPE_BASE = 10000.0
SM_SCALE = (QK_NOPE + QK_ROPE) ** -0.5
LOG2_E = 1.4426950408889634

LANE = 128
SUBLANE = 8
MXU_TILE = 256
ROW_TILE = 512
CHUNK = 256
SUB = 64
CONV_HALO = 16
ATTN_MAX_KEYS = 4224
ATTN_UNIT_KEYS = 4224
VMEM_LIMIT = 48 * 1024 * 1024


def _cparams(sem):
    return pltpu.CompilerParams(dimension_semantics=sem, vmem_limit_bytes=VMEM_LIMIT)


def _sigmoid(x):
    return 1.0 / (1.0 + jnp.exp(-x))


def _softplus(x):
    return jnp.maximum(x, 0.0) + jnp.log1p(jnp.exp(-jnp.abs(x)))


def _gelu_tanh(x):
    return 0.5 * x * (1.0 + jnp.tanh(0.7978845608028654 * (x + 0.044715 * (x * x * x))))


def _mm(a, b):
    return jnp.dot(a.astype(BF16), b.astype(BF16), preferred_element_type=F32)


def _mm_nt(a, b):
    return lax.dot_general(a.astype(BF16), b.astype(BF16), (((1,), (1,)), ((), ())),
                           preferred_element_type=F32)


def _split2(a):
    hi = a.astype(BF16)
    lo = (a - hi.astype(F32)).astype(BF16)
    return hi, lo


def _split3(a):
    hi = a.astype(BF16)
    r1 = a - hi.astype(F32)
    mid = r1.astype(BF16)
    lo = (r1 - mid.astype(F32)).astype(BF16)
    return hi, mid, lo


def _mm_exact_rhs(a, b_exact):
    hi, lo = _split2(a)
    d = functools.partial(jnp.dot, preferred_element_type=F32)
    return d(hi, b_exact) + d(lo, b_exact)


def _mm_exact_lhs(a_exact, b):
    hi, mid, lo = _split3(b)
    d = functools.partial(jnp.dot, preferred_element_type=F32)
    return d(a_exact, hi) + d(a_exact, mid) + d(a_exact, lo)


def _mm3(a, b):
    ah, al = _split2(a)
    bh, bl = _split2(b)
    d = functools.partial(jnp.dot, preferred_element_type=F32)
    return d(ah, bh) + d(ah, bl) + d(al, bh)


def _mm3_nt(a, b):
    ah, al = _split2(a)
    bh, bl = _split2(b)
    d = functools.partial(lax.dot_general, dimension_numbers=(((1,), (1,)), ((), ())),
                          preferred_element_type=F32)
    return d(ah, bh) + d(ah, bl) + d(al, bh)


def _mm_tn(a, b):
    return lax.dot_general(a.astype(BF16), b.astype(BF16), (((0,), (0,)), ((), ())),
                           preferred_element_type=F32)


def _modulate(x, shift, scale):
    ms = jnp.mean(x * x, axis=-1, keepdims=True)
    return x * lax.rsqrt(ms + EPS) * (1.0 + scale) + shift


def _ada_kernel(c_ref, w_ref, b_ref, o_ref):
    c = c_ref[...]
    s = c * _sigmoid(c)
    o_ref[...] = _mm(s, w_ref[...]) + b_ref[...]


def _ada_call(cvec, ada_w, ada_b):
    n_layers, d, nd = ada_w.shape
    tn = nd // 8
    return pl.pallas_call(
        _ada_kernel,
        out_shape=jax.ShapeDtypeStruct((n_layers, SUBLANE, nd), F32),
        grid=(n_layers, nd // tn),
        in_specs=[
            pl.BlockSpec((SUBLANE, d), lambda l, j: (0, 0)),
            pl.BlockSpec((None, d, tn), lambda l, j: (l, 0, j)),
            pl.BlockSpec((None, 1, tn), lambda l, j: (l, 0, j)),
        ],
        out_specs=pl.BlockSpec((None, SUBLANE, tn), lambda l, j: (l, 0, j)),
        compiler_params=_cparams(("parallel", "parallel")),
        name="ada_mod",
    )(cvec, ada_w, ada_b.reshape(n_layers, 1, nd))


def _ffn_kernel(*refs, row0, d_ff, chunks, split_input, mix, ctx_mix, final):
    refs = list(refs)
    o_ref = refs.pop()
    i = pl.program_id(0)
    if split_input:
        ctx_ref, lat_ref = refs.pop(0), refs.pop(0)
        x = jnp.where(i == 0, ctx_ref[...], lat_ref[...])
    else:
        x = refs.pop(0)[...]
    mod_ref = refs.pop(0)
    if mix:
        ya_ref, yb0_ref, yb1_ref, yc0_ref, yc1_ref, yd_ref = (refs.pop(0) for _ in range(6))
        yb = jnp.concatenate([yb0_ref[...], yb1_ref[...]], axis=0)
        yc = jnp.concatenate([yc0_ref[...], yc1_ref[...]], axis=0)
        yd = yd_ref[...]
        if ctx_mix:
            yd = jnp.where(i == 0, refs.pop(0)[...], yd)
        wa_ref, wb_ref, wc_ref, wd_ref = (refs.pop(0) for _ in range(4))
        d = functools.partial(jnp.dot, preferred_element_type=F32)
        y = (d(ya_ref[...], wa_ref[...]) + d(yb, wb_ref[...])
             + d(yc, wc_ref[...]) + d(yd, wd_ref[...]))
        x = x + mod_ref[5:6, :] * y
    w13_ref, w2_ref = refs.pop(0), refs.pop(0)
    xm = _modulate(x, mod_ref[row0:row0 + 1, :], mod_ref[row0 + 1:row0 + 2, :]).astype(BF16)
    acc = None
    for lo, hi in chunks:
        g = jnp.dot(xm, w13_ref[:, lo:hi], preferred_element_type=F32)
        u = jnp.dot(xm, w13_ref[:, d_ff + lo:d_ff + hi], preferred_element_type=F32)
        a = ((g * _sigmoid(g)) * u).astype(BF16)
        part = jnp.dot(a, w2_ref[lo:hi, :], preferred_element_type=F32)
        acc = part if acc is None else acc + part
    out = x + 0.5 * mod_ref[row0 + 2:row0 + 3, :] * acc
    if final:
        gain_ref = refs.pop(0)
        out = out * lax.rsqrt(jnp.mean(out * out, axis=-1, keepdims=True) + EPS) * gain_ref[...]
    o_ref[...] = out


def _group_of_tile(i, tiles_per_batch):
    return (i + tiles_per_batch - 1) // tiles_per_batch


def _ffn_call(x, mods, w13, w2, *, layer, row0, tiles_per_batch, ctx_rows, mix=None, final_gain=None):
    split_input = isinstance(x, tuple)
    final = final_gain is not None
    assert not (split_input and final)
    d = w13.shape[1]
    d_ff = w2.shape[1]
    n = ctx_rows + x[1].shape[0] if split_input else x.shape[0]
    skip = ctx_rows // ROW_TILE if final else 0
    assert d_ff % MXU_TILE == 0
    n_tiles = d_ff // MXU_TILE
    split = ((n_tiles + 1) // 2) * MXU_TILE
    chunks = ((0, split), (split, d_ff))
    resident = pl.Buffered(1)
    lat_tile = lambda i: jnp.maximum(i + skip - ctx_rows // ROW_TILE, 0)
    row_spec = lambda w: pl.BlockSpec((ROW_TILE, w), lambda i: (i + skip, 0))
    weight_spec = lambda w: pl.BlockSpec((None,) + w.shape[1:], lambda i: (layer, 0, 0), pipeline_mode=resident)

    in_specs, args = [], []
    if split_input:
        in_specs += [pl.BlockSpec((ROW_TILE, d), lambda i: (0, 0)),
                     pl.BlockSpec((ROW_TILE, d), lambda i: (lat_tile(i), 0))]
        args += list(x)
    else:
        in_specs.append(row_spec(d))
        args.append(x)
    in_specs.append(pl.BlockSpec((None, N_MOD, d),
                                 lambda i: (_group_of_tile(i + skip, tiles_per_batch), 0, 0)))
    args.append(mods)
    ctx_mix = False
    if mix is not None:
        ya, yb, yc, yd_lat, yd_ctx, wo_parts = mix
        assert ROW_TILE == 2 * CHUNK
        n_b = ctx_rows // CHUNK
        cpb = tiles_per_batch * (ROW_TILE // CHUNK)
        pb_half = lambda y, half: pl.BlockSpec(
            (CHUNK, y.shape[1]), lambda i: (_pb_chunk(2 * (i + skip) + half, n_b, cpb), 0))
        in_specs += [row_spec(ya.shape[1]), pb_half(yb, 0), pb_half(yb, 1), pb_half(yc, 0), pb_half(yc, 1),
                     pl.BlockSpec((ROW_TILE, yd_lat.shape[1]), lambda i: (lat_tile(i), 0))]
        args += [ya, yb, yb, yc, yc, yd_lat]
        ctx_mix = yd_ctx is not None
        if ctx_mix:
            in_specs.append(pl.BlockSpec((ROW_TILE, yd_ctx.shape[1]), lambda i: (0, 0)))
            args.append(yd_ctx)
        in_specs += [weight_spec(w) for w in wo_parts]
        args += list(wo_parts)
    in_specs += [weight_spec(w13), weight_spec(w2)]
    args += [w13, w2]
    if final:
        in_specs.append(pl.BlockSpec((1, d), lambda i: (0, 0)))
        args.append(final_gain.reshape(1, d))
    kern = functools.partial(_ffn_kernel, row0=row0, d_ff=d_ff, chunks=chunks, split_input=split_input,
                             mix=mix is not None, ctx_mix=ctx_mix, final=final)
    out_rows = n - skip * ROW_TILE
    return pl.pallas_call(
        kern,
        out_shape=jax.ShapeDtypeStruct((out_rows, d), F32),
        grid=(out_rows // ROW_TILE,),
        in_specs=in_specs,
        out_specs=pl.BlockSpec((ROW_TILE, d), lambda i: (i, 0)),
        compiler_params=_cparams(("parallel",)),
        name="ffn_mix" if mix is not None else "ffn",
    )(*args)


def _win_kernel(x_ref, mod_ref, wa_ref, wb_ref, wc_ref, wd_ref, cos_ref, sin_ref, qn_ref, kvn_ref,
                wqm_ref, wqr_ref, wk_ref, wv_ref, oa_ref, ob_ref, oc_ref, q_ref, k_ref, v_ref):
    xm = _modulate(x_ref[...], mod_ref[3:4, :], mod_ref[4:5, :]).astype(BF16)
    oa_ref[...] = jnp.dot(xm, wa_ref[...], preferred_element_type=F32)
    ob_ref[...] = jnp.dot(xm, wb_ref[...], preferred_element_type=F32)
    oc_ref[...] = jnp.dot(xm, wc_ref[...], preferred_element_type=F32)
    u_d = jnp.dot(xm, wd_ref[...], preferred_element_type=F32)
    _mla_project(u_d, cos_ref, sin_ref, qn_ref, kvn_ref, wqm_ref, wqr_ref, wk_ref, wv_ref, q_ref, k_ref, v_ref)


def _win_call(x, mods, wa, wb, wc, wd, cos_tab, sin_tab, mla_params, *, layer, n_batch, cpb):
    n, d = x.shape
    hw = MLA_HEADS * LANE
    widths = (wa.shape[2], wb.shape[2], wc.shape[2])
    group = lambda c: jnp.where(c < n_batch, 0, 1 + jnp.maximum(c - n_batch, 0) // cpb)
    flat = lambda c: (c, 0)
    per_batch = lambda c: (_pb_chunk(c, n_batch, cpb), 0)
    kv_cpb = cpb + 1

    def kv_chunk(c):
        lat = jnp.maximum(c - n_batch, 0)
        bb = lat // cpb
        return (jnp.where(c < n_batch, c * kv_cpb + cpb, bb * kv_cpb + (lat - bb * cpb)), 0)

    def tab_chunk(c):
        return (jnp.where(c < n_batch, 0, 1 + lax.rem(jnp.maximum(c - n_batch, 0), cpb)), 0)

    out_maps = (flat, per_batch, per_batch, flat, kv_chunk, kv_chunk)
    out_widths = widths + (hw, hw, hw)
    out_dtypes = (F32, F32, F32, BF16, BF16, BF16)
    return pl.pallas_call(
        _win_kernel,
        out_shape=tuple(jax.ShapeDtypeStruct((n, w), dt) for w, dt in zip(out_widths, out_dtypes)),
        grid=(n // CHUNK,),
        in_specs=[
            pl.BlockSpec((CHUNK, d), flat),
            pl.BlockSpec((None, N_MOD, d), lambda c: (group(c), 0, 0)),
        ] + [_param_spec(w, (layer,)) for w in (wa, wb, wc, wd)]
          + [pl.BlockSpec((CHUNK, LANE), tab_chunk), pl.BlockSpec((CHUNK, LANE), tab_chunk)]
          + [_param_spec(arr, (layer,)) for arr in mla_params],
        out_specs=tuple(pl.BlockSpec((CHUNK, w), m) for w, m in zip(out_widths, out_maps)),
        compiler_params=_cparams(("parallel",)),
        name="w_in",
    )(x, mods, wa, wb, wc, wd, cos_tab, sin_tab, *mla_params)


def _seq_flags(c, n_batch, cpb):
    j = lax.rem(jnp.maximum(c - n_batch, 0), cpb)
    is_ctx = c < n_batch
    first = jnp.logical_or(is_ctx, j == 0)
    last = jnp.logical_or(is_ctx, j == cpb - 1)
    return first, last


def _pb_chunk(c, n_batch, cpb):
    lat = jnp.maximum(c - n_batch, 0)
    b = lat // cpb
    return jnp.where(c < n_batch, c * (cpb + 1), b * (cpb + 1) + 1 + (lat - b * cpb))


def _pb_scan_chunk(i, cpb, reverse):
    if reverse:
        return jnp.where(i == 0, 0, cpb + 1 - i)
    return i


def _conv_kernel(cur_ref, prev_ref, next_ref, w_ref, b_ref, g_ref, beta_ref, o_ref, zbuf, zrot,
                 *, n_batch, cpb):
    c = pl.program_id(0)
    first, last = _seq_flags(c, n_batch, cpb)

    def glu(u):
        return u[:, :GROUP_W] * _sigmoid(u[:, GROUP_W:])

    zp = glu(prev_ref[...])
    zn = glu(next_ref[...])
    zbuf[0:CONV_HALO, :] = jnp.where(first, 0.0, zp)
    zbuf[CONV_HALO:CONV_HALO + CHUNK, :] = glu(cur_ref[...])
    zbuf[CONV_HALO + CHUNK:2 * CONV_HALO + CHUNK, :] = jnp.where(last, 0.0, zn)

    pad = CONV_K // 2
    span = CHUNK + 2 * CONV_HALO - SUBLANE
    acc = jnp.zeros((CHUNK, GROUP_W), F32) + b_ref[...]
    for phase in range(SUBLANE):
        if phase:
            zrot[phase - 1] = zbuf[phase:phase + span, :]
        for j in range(CONV_K):
            start = CONV_HALO - pad + j
            if start % SUBLANE == phase:
                lo = start - phase
                win = zrot[phase - 1, lo:lo + CHUNK, :] if phase else zbuf[lo:lo + CHUNK, :]
                acc = acc + w_ref[j:j + 1, :] * win
    mu = jnp.mean(acc, axis=-1, keepdims=True)
    xc = acc - mu
    var = jnp.mean(xc * xc, axis=-1, keepdims=True)
    y = xc * lax.rsqrt(var + LN_EPS) * g_ref[...] + beta_ref[...]
    o_ref[...] = (y * _sigmoid(y)).astype(o_ref.dtype)


def _conv_call(u_a, dw_w, dw_b, ln_g, ln_b, *, layer, n_batch, cpb):
    n = u_a.shape[0]
    n_chunks = n // CHUNK
    hpc = CHUNK // CONV_HALO
    n_halo = n // CONV_HALO
    kern = functools.partial(_conv_kernel, n_batch=n_batch, cpb=cpb)
    return pl.pallas_call(
        kern,
        out_shape=jax.ShapeDtypeStruct((n, GROUP_W), BF16),
        grid=(n_chunks,),
        in_specs=[
            pl.BlockSpec((CHUNK, 2 * GROUP_W), lambda c: (c, 0)),
            pl.BlockSpec((CONV_HALO, 2 * GROUP_W), lambda c: (jnp.maximum(c * hpc - 1, 0), 0)),
            pl.BlockSpec((CONV_HALO, 2 * GROUP_W),
                         lambda c: (jnp.minimum((c + 1) * hpc, n_halo - 1), 0)),
        ] + [_param_spec(arr, (layer,)) for arr in (dw_w, dw_b, ln_g, ln_b)],
        out_specs=pl.BlockSpec((CHUNK, GROUP_W), lambda c: (c, 0)),
        scratch_shapes=[pltpu.VMEM((CHUNK + 2 * CONV_HALO, GROUP_W), F32),
                        pltpu.VMEM((SUBLANE - 1, CHUNK + 2 * CONV_HALO - SUBLANE, GROUP_W), F32)],
        compiler_params=_cparams(("parallel",)),
        name="conformer_conv",
    )(u_a, u_a, u_a, dw_w, dw_b, ln_g, ln_b)


def _lru_kernel(*refs, n_batch, cpb, reverse):
    if reverse:
        cur_ref, prev_ref, next_ref, cw_ref, cb_ref, wg_ref, bg_ref, lam_ref, o_ref, xbuf, carry = refs
        hrev_ref = None
    else:
        (cur_ref, prev_ref, next_ref, cw_ref, cb_ref, wg_ref, bg_ref, lam_ref, hrev_ref,
         o_ref, xbuf, carry) = refs
    i = pl.program_id(0)
    jc = _pb_scan_chunk(i, cpb, reverse)
    first = jnp.logical_or(i == 0, jc == 1)
    last = jnp.logical_or(i == 0, jc == cpb)
    rows_all = n_batch * CHUNK

    @pl.when(i == 0)
    def _():
        carry[...] = jnp.zeros_like(carry)

    u = cur_ref[...]
    gb = u[:, :, GROUP_W:].reshape(rows_all, GROUP_W)
    xbuf[:, 0:SUBLANE, :] = jnp.where(first, 0.0, prev_ref[:, :, :GROUP_W])
    xbuf[:, SUBLANE:SUBLANE + CHUNK, :] = u[:, :, :GROUP_W]
    xbuf[:, SUBLANE + CHUNK:2 * SUBLANE + CHUNK, :] = jnp.where(last, 0.0, next_ref[:, :, :GROUP_W])
    pad_l = LRU_CONV_K // 2
    xv = jnp.zeros((rows_all, GROUP_W), F32) + cb_ref[...]
    for j in range(LRU_CONV_K):
        start = SUBLANE - pad_l + j
        xv = xv + cw_ref[j:j + 1, :] * xbuf[:, start:start + CHUNK, :].reshape(rows_all, GROUP_W)

    z = _mm(xv, wg_ref[...]) + bg_ref[...]
    r = _sigmoid(z[:, :GROUP_W])
    ig = _sigmoid(z[:, GROUP_W:])
    log_a = (-LRU_C) * r * _softplus(-lam_ref[...])
    a = jnp.exp(log_a)
    t = jnp.tanh(log_a)
    om = -2.0 * t / (1.0 - t)
    bb = jnp.where(om > 0.0, om * lax.rsqrt(om), 0.0) * (ig * xv)

    groups = CHUNK // SUBLANE
    g3 = (rows_all // SUBLANE, SUBLANE, GROUP_W)
    a = a.reshape(g3)
    bb = bb.reshape(g3)
    row = lax.broadcasted_iota(jnp.int32, g3, 1)
    s = 1
    while s < SUBLANE:
        if reverse:
            a_sh = pltpu.roll(a, SUBLANE - s, axis=1)
            b_sh = pltpu.roll(bb, SUBLANE - s, axis=1)
            valid = row < SUBLANE - s
        else:
            a_sh = pltpu.roll(a, s, axis=1)
            b_sh = pltpu.roll(bb, s, axis=1)
            valid = row >= s
        bb = jnp.where(valid, a * b_sh + bb, bb)
        a = jnp.where(valid, a * a_sh, a)
        s *= 2
    a = a.reshape(rows_all, GROUP_W)
    bb = bb.reshape(rows_all, GROUP_W)
    state = [carry[bi] for bi in range(n_batch)]
    h_parts = {}
    edge_row = 0 if reverse else SUBLANE - 1
    for g in (range(groups - 1, -1, -1) if reverse else range(groups)):
        for bi in range(n_batch):
            r0 = bi * CHUNK + g * SUBLANE
            hg = bb[r0:r0 + SUBLANE, :] + a[r0:r0 + SUBLANE, :] * state[bi]
            h_parts[(bi, g)] = hg
            state[bi] = jnp.broadcast_to(hg[edge_row:edge_row + 1, :], (SUBLANE, GROUP_W))
    for bi in range(n_batch):
        carry[bi] = state[bi]
    h = jnp.concatenate([h_parts[(bi, g)] for bi in range(n_batch) for g in range(groups)], axis=0)

    if reverse:
        o_ref[...] = h.reshape(n_batch, CHUNK, GROUP_W)
    else:
        out = (h + hrev_ref[...].reshape(rows_all, GROUP_W)) * _gelu_tanh(gb)
        o_ref[...] = out.astype(o_ref.dtype).reshape(n_batch, CHUNK, GROUP_W)


def _lru_call(u_b, conv_w, conv_b, w_gate, b_gate, lam, h_rev, *, layer, n_batch, cpb, reverse):
    n, wb = u_b.shape
    bpc = CHUNK // SUBLANE
    cps = cpb + 1
    assert n == n_batch * cps * CHUNK
    chunk = lambda i: _pb_scan_chunk(i, cpb, reverse)
    u4 = u_b.reshape(n_batch, cps, CHUNK, wb)
    u8 = u_b.reshape(n_batch, cps * bpc, SUBLANE, wb)
    d = 1 if reverse else 0
    seq_spec = pl.BlockSpec((n_batch, None, CHUNK, GROUP_W), lambda i: (0, chunk(i), 0, 0))
    in_specs = [
        pl.BlockSpec((n_batch, None, CHUNK, wb), lambda i: (0, chunk(i), 0, 0)),
        pl.BlockSpec((n_batch, None, SUBLANE, wb), lambda i: (0, jnp.maximum(chunk(i) * bpc - 1, 0), 0, 0)),
        pl.BlockSpec((n_batch, None, SUBLANE, wb),
                     lambda i: (0, jnp.minimum((chunk(i) + 1) * bpc, cps * bpc - 1), 0, 0)),
        _param_spec(conv_w, (layer,)), _param_spec(conv_b, (layer,)), _param_spec(w_gate, (layer, d)),
        _param_spec(b_gate, (layer, d)), _param_spec(lam, (layer, d)),
    ]
    args = [u4, u8, u8, conv_w, conv_b, w_gate, b_gate, lam]
    if not reverse:
        in_specs.append(seq_spec)
        args.append(h_rev)
    kern = functools.partial(_lru_kernel, n_batch=n_batch, cpb=cpb, reverse=reverse)
    return dict(kernel=kern, in_specs=in_specs, args=args,
                out_shape=[jax.ShapeDtypeStruct((n_batch, cps, CHUNK, GROUP_W), F32 if reverse else BF16)],
                out_specs=[seq_spec],
                scratch=[pltpu.VMEM((n_batch, CHUNK + 2 * SUBLANE, GROUP_W), F32),
                         pltpu.VMEM((n_batch, SUBLANE, GROUP_W), F32)])


def _rwkv_kernel(*refs, n_batch, cpb, reverse):
    if reverse:
        (cur_ref, prev_ref, next_ref, mup_ref, mun_ref, w0_ref, wup_ref, a0_ref, aup_ref, gup_ref,
         kk_ref, ka_ref, rk_ref, gng_ref, gnb_ref, hones_ref,
         y_ref, bonus_ref, ubuf, st_scr) = refs
        yrev_ref = brev_ref = None
    else:
        (cur_ref, prev_ref, next_ref, mup_ref, mun_ref, w0_ref, wup_ref, a0_ref, aup_ref, gup_ref,
         kk_ref, ka_ref, rk_ref, gng_ref, gnb_ref, hones_ref, yrev_ref, brev_ref,
         y_ref, ubuf, st_scr) = refs
    i = pl.program_id(0)
    j = _pb_scan_chunk(i, cpb, reverse)
    first = jnp.logical_or(i == 0, j == 1)
    last = jnp.logical_or(i == 0, j == cpb)
    g4 = RWKV_HEADS * RWKV_HEAD
    assert g4 == GROUP_W and RWKV_HEADS * SUB == GROUP_W
    rows_all = n_batch * CHUNK
    wc = cur_ref.shape[-1]

    @pl.when(i == 0)
    def _():
        st_scr[...] = jnp.zeros_like(st_scr)

    ubuf[:, 0:SUBLANE, :] = jnp.where(first, 0.0, prev_ref[...])
    ubuf[:, SUBLANE:SUBLANE + CHUNK, :] = cur_ref[...]
    ubuf[:, SUBLANE + CHUNK:2 * SUBLANE + CHUNK, :] = jnp.where(last, 0.0, next_ref[...])
    u = cur_ref[...].reshape(rows_all, wc)
    up = ubuf[:, SUBLANE - 1:SUBLANE - 1 + CHUNK, :].reshape(rows_all, wc)
    un = ubuf[:, SUBLANE + 1:SUBLANE + 1 + CHUNK, :].reshape(rows_all, wc)
    vs = u + mup_ref[...] * (up - u) + mun_ref[...] * (un - u)

    hones = hones_ref[...]
    r = vs[:, 0:GROUP_W]
    k = vs[:, GROUP_W:2 * GROUP_W]
    val = vs[:, 2 * GROUP_W:3 * GROUP_W]
    lora_in = vs[:, 3 * GROUP_W:3 * GROUP_W + LANE]
    gate_in = vs[:, 3 * GROUP_W + LANE:]
    w_lin = _mm(jnp.tanh(lora_in), wup_ref[...])
    a_lin = _mm(lora_in, aup_ref[...])
    kq = k * kk_ref[...]
    ss = _mm_exact_rhs(kq * kq, hones)
    kk = kq * lax.rsqrt(jnp.maximum(ss, 1e-24))
    lw = -jnp.exp(-_softplus(-(w0_ref[...] + w_lin)) - 0.5)
    a = _sigmoid(a0_ref[...] + a_lin)
    kd = k * (1.0 + (a - 1.0) * ka_ref[...])
    bonus = _mm_exact_rhs(r * kd * rk_ref[...], hones) * val

    bvec = kk * a

    ri = lax.broadcasted_iota(jnp.int32, (g4, g4), 0)
    ci = lax.broadcasted_iota(jnp.int32, (g4, g4), 1)
    same = (ri // SUB) == (ci // SUB)
    rt = lax.rem(ri, SUB)
    ct = lax.rem(ci, SUB)
    if reverse:
        strict = jnp.logical_and(same, ct > rt)
        incl = jnp.logical_and(same, ct >= rt)
    else:
        strict = jnp.logical_and(same, ct < rt)
        incl = jnp.logical_and(same, ct <= rt)
    eye = ri == ci
    ti = lax.broadcasted_iota(jnp.int32, (SUB, SUB), 0)
    tj = lax.broadcasted_iota(jnp.int32, (SUB, SUB), 1)
    tri = jnp.where((tj >= ti) if reverse else (tj <= ti), 1.0, 0.0).astype(BF16)
    lane_head = lax.broadcasted_iota(jnp.int32, (1, g4), 1) // RWKV_HEAD

    def stack(x):
        return jnp.concatenate([jnp.where(lane_head == h, x, 0.0) for h in range(RWKV_HEADS)], axis=0)

    def unstack(x):
        out = x[0:SUB, :]
        for h in range(1, RWKV_HEADS):
            out = out + x[h * SUB:(h + 1) * SUB, :]
        return out

    per_batch = CHUNK // SUB
    n_sub = n_batch * per_batch
    subs = range(n_sub)
    rows = [slice(sc * SUB, (sc + 1) * SUB) for sc in subs]
    each = lambda f, *lists: [f(*xs) for xs in zip(*lists)]
    lw_s = [lw[rw] for rw in rows]
    kk_s = [kk[rw] for rw in rows]
    b_s = [bvec[rw] for rw in rows]
    kd_s = [kd[rw] for rw in rows]
    r_s = [r[rw] for rw in rows]
    v_s = [val[rw] for rw in rows]

    cum = each(lambda x: _mm_exact_lhs(tri, x), lw_s)
    edge = each(lambda x: x[0:1, :] if reverse else x[SUB - 1:SUB, :], cum)
    a_st = each(lambda cu, l, x: stack(jnp.exp(cu - l) * x), cum, lw_s, kk_s)
    b_st = each(lambda cu, x: stack(x * jnp.exp(-cu)), cum, b_s)
    k_st = each(lambda cu, x: stack(x * jnp.exp(-cu)), cum, kd_s)
    r_st = each(lambda cu, x: stack(x * jnp.exp(cu)), cum, r_s)
    v_bf = each(lambda x: stack(x).astype(BF16), v_s)
    e_out = each(lambda ed, cu: jnp.exp(ed - cu), edge, cum)
    beta_w = each(lambda x, e: x * e, b_s, e_out)
    kappa_w = each(lambda x, e: x * e, kd_s, e_out)
    w_all = each(jnp.exp, edge)

    prod = each(lambda a_, r_, b_, k_: _mm_nt(jnp.concatenate([a_, r_], axis=0),
                                              jnp.concatenate([b_, k_], axis=0)),
                a_st, r_st, b_st, k_st)
    l_ak = each(lambda p_: jnp.where(strict, p_[0:g4, g4:], 0.0), prod)
    m_rb = each(lambda p_: jnp.where(incl, p_[g4:, 0:g4], 0.0).astype(BF16), prod)
    m_rk = each(lambda p_: jnp.where(incl, p_[g4:, g4:], 0.0), prod)

    l_ab = each(lambda p_: jnp.where(strict, p_[0:g4, 0:g4], 0.0), prod)
    q = each(lambda l_: -l_, l_ab)
    tm = q
    for _ in range(5):
        q = each(lambda q_: _mm(q_, q_), q)
        tm = each(lambda t_, q_: t_ + q_ + _mm(t_, q_), tm, q)

    def refine(t_, l_):
        res = -(t_ + l_) - _mm3(l_, t_)
        return t_ + res + _mm(t_, res)

    tm = each(refine, tm, l_ab)

    lakv = each(_mm, l_ak, v_bf)
    rhs = each(lambda a_, x: jnp.concatenate([a_, x], axis=1), a_st, lakv)
    sol = each(lambda t_, x: x + _mm(t_, x), tm, rhs)
    corr = each(_mm, m_rb, sol)
    ra_st = each(lambda r_, c_: r_ - c_[:, 0:g4], r_st, corr)
    y0_st = each(lambda m_, v_, c_: _mm(m_, v_) - c_[:, g4:], m_rk, v_bf, corr)
    ta = each(lambda s_: unstack(s_[:, 0:g4]), sol)
    z0 = each(lambda s_: unstack(s_[:, g4:]), sol)
    gt = each(lambda w_, bw, ta_: jnp.where(eye, w_, 0.0) - jnp.where(same, _mm_tn(bw, ta_), 0.0),
              w_all, beta_w, ta)
    ht = each(lambda kw, v_, bw, z_: jnp.where(same, _mm_tn(kw, v_) - _mm_tn(bw, z_), 0.0),
              kappa_w, v_s, beta_w, z0)

    order = list(range(per_batch - 1, -1, -1)) if reverse else list(range(per_batch))
    st = [st_scr[bi] for bi in range(n_batch)]
    y_parts = {}
    for t in order:
        for bi in range(n_batch):
            sc = bi * per_batch + t
            y_parts[sc] = unstack(_mm(ra_st[sc], st[bi]) + y0_st[sc])
            st[bi] = _mm3(gt[sc], st[bi]) + ht[sc]
    for bi in range(n_batch):
        st_scr[bi] = st[bi]
    y_all = jnp.concatenate([y_parts[sc] for sc in subs], axis=0)

    if reverse:
        y_ref[...] = y_all.reshape(n_batch, CHUNK, GROUP_W)
        bonus_ref[...] = bonus.reshape(n_batch, CHUNK, GROUP_W)
    else:
        yy = y_all + yrev_ref[...].reshape(rows_all, GROUP_W)
        inv_n = 1.0 / RWKV_HEAD
        mu = _mm_exact_rhs(yy, hones) * inv_n
        yc = yy - mu
        var = _mm_exact_rhs(yc * yc, hones) * inv_n
        o = (yc * lax.rsqrt(var + GN_EPS) * gng_ref[...] + gnb_ref[...] + bonus
             + brev_ref[...].reshape(rows_all, GROUP_W))
        gate = _mm(_sigmoid(gate_in), gup_ref[...])
        y_ref[...] = (o * gate).astype(y_ref.dtype).reshape(n_batch, CHUNK, GROUP_W)


def _param_spec(arr, lead):
    shape = (None,) * len(lead) + arr.shape[len(lead):]
    index = tuple(lead) + (0,) * (arr.ndim - len(lead))
    return pl.BlockSpec(shape, lambda *_: index)


def _rwkv_params(mu_prev, mu_next, w0, w_up, a0, a_up, g_up, k_k, k_a, r_k, gn_g, gn_b):
    depth = mu_prev.shape[0]
    row = lambda v: v.reshape(v.shape[:-1] + (1, v.shape[-1]))
    zeros64 = jnp.zeros(w_up.shape[:2] + (RWKV_HEAD, GROUP_W), F32)
    hid = jnp.arange(GROUP_W) // RWKV_HEAD
    return dict(
        mu_prev=row(mu_prev), mu_next=row(mu_next), w0=row(w0), a0=row(a0),
        w_up=jnp.concatenate([w_up, zeros64], axis=2).astype(BF16),
        a_up=jnp.concatenate([zeros64, a_up], axis=2).astype(BF16),
        g_up=g_up.astype(BF16), k_k=row(k_k), k_a=row(k_a), r_k=row(r_k.reshape(depth, GROUP_W)),
        gn_g=row(gn_g), gn_b=row(gn_b),
        hones=(hid[:, None] == hid[None, :]).astype(BF16))


def _rwkv_call(u_c, p, y_rev, bonus_rev, *, layer, n_batch, cpb, reverse):
    n, wc = u_c.shape
    bpc = CHUNK // SUBLANE
    cps = cpb + 1
    assert n == n_batch * cps * CHUNK
    chunk = lambda i: _pb_scan_chunk(i, cpb, reverse)
    u4 = u_c.reshape(n_batch, cps, CHUNK, wc)
    u8 = u_c.reshape(n_batch, cps * bpc, SUBLANE, wc)
    d = 1 if reverse else 0
    small = [(p['mu_prev'], (layer,)), (p['mu_next'], (layer,)), (p['w0'], (layer, d)), (p['w_up'], (layer, d)),
             (p['a0'], (layer, d)), (p['a_up'], (layer, d)), (p['g_up'], (layer,)), (p['k_k'], (layer,)),
             (p['k_a'], (layer,)), (p['r_k'], (layer,)), (p['gn_g'], (layer,)), (p['gn_b'], (layer,)),
             (p['hones'], ())]
    in_specs = [
        pl.BlockSpec((n_batch, None, CHUNK, wc), lambda i: (0, chunk(i), 0, 0)),
        pl.BlockSpec((n_batch, None, SUBLANE, wc), lambda i: (0, jnp.maximum(chunk(i) * bpc - 1, 0), 0, 0)),
        pl.BlockSpec((n_batch, None, SUBLANE, wc),
                     lambda i: (0, jnp.minimum((chunk(i) + 1) * bpc, cps * bpc - 1), 0, 0)),
    ] + [_param_spec(arr, lead) for arr, lead in small]
    args = [u4, u8, u8] + [arr for arr, _ in small]
    seq_spec = pl.BlockSpec((n_batch, None, CHUNK, GROUP_W), lambda i: (0, chunk(i), 0, 0))
    seq_shape = (n_batch, cps, CHUNK, GROUP_W)
    scratch = [pltpu.VMEM((n_batch, CHUNK + 2 * SUBLANE, wc), F32),
               pltpu.VMEM((n_batch, GROUP_W, GROUP_W), F32)]
    if reverse:
        out_shape = [jax.ShapeDtypeStruct(seq_shape, F32), jax.ShapeDtypeStruct(seq_shape, F32)]
        out_specs = [seq_spec, seq_spec]
    else:
        in_specs += [seq_spec, seq_spec]
        args += [y_rev, bonus_rev]
        out_shape = [jax.ShapeDtypeStruct(seq_shape, BF16)]
        out_specs = [seq_spec]
    kern = functools.partial(_rwkv_kernel, n_batch=n_batch, cpb=cpb, reverse=reverse)
    return dict(kernel=kern, in_specs=in_specs, args=args, out_shape=out_shape, out_specs=out_specs,
                scratch=scratch)


def _scan_mixers_kernel(*refs, parts):
    n_in = [len(p['in_specs']) for p in parts]
    n_out = [len(p['out_specs']) for p in parts]
    n_scr = [len(p['scratch']) for p in parts]
    ins, outs, scr = [], [], []
    pos = 0
    for k in n_in:
        ins.append(refs[pos:pos + k])
        pos += k
    for k in n_out:
        outs.append(refs[pos:pos + k])
        pos += k
    for k in n_scr:
        scr.append(refs[pos:pos + k])
        pos += k
    for p, i_, o_, s_ in reversed(list(zip(parts, ins, outs, scr))):
        p['kernel'](*i_, *o_, *s_)


def _scan_mixers_call(parts, *, n_steps, name):
    flat = lambda key: [x for p in parts for x in p[key]]
    static = [dict(kernel=p['kernel'], in_specs=p['in_specs'], out_specs=p['out_specs'], scratch=p['scratch'])
              for p in parts]
    outs = pl.pallas_call(
        functools.partial(_scan_mixers_kernel, parts=static),
        out_shape=tuple(flat('out_shape')),
        grid=(n_steps,),
        in_specs=flat('in_specs'),
        out_specs=tuple(flat('out_specs')),
        scratch_shapes=flat('scratch'),
        compiler_params=_cparams(("arbitrary",)),
        name=name,
    )(*flat('args'))
    return list(outs)


def _mla_project(u, cos_ref, sin_ref, qn_ref, kvn_ref, wqm_ref, wqr_ref, wk_ref, wv_ref, q_ref, k_ref, v_ref):
    cq = u[:, :Q_LORA]
    ckv = u[:, Q_LORA:Q_LORA + KV_LORA]
    blk_r = u[:, Q_LORA + KV_LORA:Q_LORA + KV_LORA + LANE]
    blk_rr = u[:, Q_LORA + KV_LORA + LANE:]
    cqn = (cq * lax.rsqrt(jnp.mean(cq * cq, axis=-1, keepdims=True) + EPS) * qn_ref[...]).astype(BF16)
    ckvn = (ckv * lax.rsqrt(jnp.mean(ckv * ckv, axis=-1, keepdims=True) + EPS) * kvn_ref[...]).astype(BF16)
    cos_t = cos_ref[...]
    sin_t = sin_ref[...]
    cos4 = jnp.concatenate([cos_t] * MLA_HEADS, axis=1)
    sin4 = jnp.concatenate([sin_t] * MLA_HEADS, axis=1)
    qm = jnp.dot(cqn, wqm_ref[...], preferred_element_type=F32)
    qr = jnp.dot(cqn, wqr_ref[...], preferred_element_type=F32)
    q_ref[...] = ((qm * cos4 + qr * sin4) * (SM_SCALE * LOG2_E)).astype(q_ref.dtype)
    kr = blk_r * cos_t + blk_rr * sin_t
    km = jnp.dot(ckvn, wk_ref[...], preferred_element_type=F32)
    k_ref[...] = (km + jnp.concatenate([kr] * MLA_HEADS, axis=1)).astype(k_ref.dtype)
    lane = lax.broadcasted_iota(jnp.int32, (1, MLA_HEADS * LANE), 1)
    ones_col = jnp.where(lax.rem(lane, LANE) == V_HEAD, 1.0, 0.0)
    v_ref[...] = (jnp.dot(ckvn, wv_ref[...], preferred_element_type=F32) + ones_col).astype(v_ref.dtype)


def _attn_kernel(q_ref, k_ref, v_ref, o_ref, m_scr, acc_scr, *, n_kv, unit_keys):
    kj = pl.program_id(2)

    @pl.when(kj == 0)
    def _():
        m_scr[...] = jnp.full_like(m_scr, -jnp.inf)
        acc_scr[...] = jnp.zeros_like(acc_scr)

    tk = k_ref.shape[0]
    bounds = list(range(0, tk, unit_keys)) + [tk]
    units = [(h, lo, hi) for h in range(MLA_HEADS) for lo, hi in zip(bounds[:-1], bounds[1:])]

    def scores(u):
        h, lo, hi = u
        sl = slice(h * LANE, (h + 1) * LANE)
        return lax.dot_general(q_ref[:, sl], k_ref[lo:hi, sl], (((1,), (1,)), ((), ())),
                               preferred_element_type=F32)

    def softmax_step(u, s):
        h = u[0]
        m_prev = m_scr[h]
        m_new = jnp.maximum(m_prev, jnp.max(s, axis=-1, keepdims=True))
        m_scr[h] = m_new
        return jnp.exp2(s - m_new).astype(BF16), jnp.exp2(m_prev - m_new)

    def weighted_values(u, p, alpha):
        h, lo, hi = u
        sl = slice(h * LANE, (h + 1) * LANE)
        acc_scr[h] = alpha * acc_scr[h] + jnp.dot(p, v_ref[lo:hi, sl], preferred_element_type=F32)

    s_next = scores(units[0])
    pending = None
    for idx, u in enumerate(units):
        s_cur = s_next
        if idx + 1 < len(units):
            s_next = scores(units[idx + 1])
        p, alpha = softmax_step(u, s_cur)
        if pending is not None:
            weighted_values(*pending)
        pending = (u, p, alpha)
    weighted_values(*pending)

    @pl.when(kj == n_kv - 1)
    def _():
        for h in range(MLA_HEADS):
            sl = slice(h * LANE, (h + 1) * LANE)
            acc = acc_scr[h]
            o_ref[:, sl] = (acc / acc[:, V_HEAD:V_HEAD + 1]).astype(o_ref.dtype)


def _attn_call(q, k, v, *, out_rows, n_batch, tq, tk, n_q, n_kv, q_block, kv_block, o_block, name):
    hw = q.shape[1]
    unit = min(ATTN_UNIT_KEYS, tk)
    kern = functools.partial(_attn_kernel, n_kv=n_kv, unit_keys=unit)
    return pl.pallas_call(
        kern,
        out_shape=jax.ShapeDtypeStruct((out_rows, hw), BF16),
        grid=(n_batch, n_q, n_kv),
        in_specs=[
            pl.BlockSpec((tq, hw), lambda b, qi, kj: (q_block(b, qi), 0)),
            pl.BlockSpec((tk, hw), lambda b, qi, kj: (kv_block(b, kj), 0)),
            pl.BlockSpec((tk, hw), lambda b, qi, kj: (kv_block(b, kj), 0)),
        ],
        out_specs=pl.BlockSpec((tq, hw), lambda b, qi, kj: (o_block(b, qi), 0)),
        scratch_shapes=[pltpu.VMEM((MLA_HEADS, tq, 1), F32), pltpu.VMEM((MLA_HEADS, tq, LANE), F32)],
        compiler_params=_cparams(("parallel", "parallel", "arbitrary")),
        name=name,
    )(q, k, v)


def _kv_tile(n_keys):
    best = LANE
    t = LANE
    while t <= ATTN_MAX_KEYS:
        if n_keys % t == 0:
            best = t
        t += LANE
    return best


def _rot_cols(w):
    q = QK_ROPE // 4
    return jnp.concatenate([-w[..., q:2 * q], w[..., 0:q], -w[..., 3 * q:4 * q], w[..., 2 * q:3 * q]],
                           axis=-1)


def _block_diag(w):
    nb, n = w.shape[-3], w.shape[-2]
    eye = jnp.eye(nb, dtype=w.dtype)
    out = eye[:, None, :, None] * w[..., :, :, None, :]
    return out.reshape(w.shape[:-3] + (nb * n, nb * n))


def _rope_tables(t_len, ctx_len):
    t = jnp.arange(t_len, dtype=jnp.int32)
    rows = (t // GRID_W).astype(F32)
    cols = (t % GRID_W).astype(F32)
    n_freq = QK_ROPE // 4
    inv_freq = ROPE_BASE ** (-jnp.arange(n_freq, dtype=F32) / n_freq)
    ang = jnp.stack([rows[:, None] * inv_freq, cols[:, None] * inv_freq], axis=1)
    ang = jnp.concatenate([ang, ang], axis=-1).reshape(t_len, QK_ROPE)
    cos = jnp.concatenate([jnp.ones((ctx_len, QK_ROPE), F32), jnp.cos(ang)], axis=0)
    sin = jnp.concatenate([jnp.zeros((ctx_len, QK_ROPE), F32), jnp.sin(ang)], axis=0)
    n = t_len + ctx_len
    pad = LANE - QK_NOPE - QK_ROPE
    cos_tab = jnp.concatenate([jnp.ones((n, QK_NOPE), F32), cos, jnp.zeros((n, pad), F32)], axis=1)
    sin_tab = jnp.concatenate([jnp.zeros((n, QK_NOPE), F32), sin, jnp.zeros((n, pad), F32)], axis=1)
    return cos_tab, sin_tab


def kernel(x, c, ctx, c_ctx, ada_w, ada_b, ffn1_w13, ffn1_w2, ffn2_w13, ffn2_w2, w_in, w_out,
           cv_dw_w, cv_dw_b, cv_ln_g, cv_ln_b,
           lru_conv_w, lru_conv_b, lru_wa, lru_ba, lru_wx, lru_bx, lru_lambda,
           rwkv_mu_prev, rwkv_mu_next, rwkv_w0, rwkv_w_up, rwkv_a0, rwkv_a_up, rwkv_g_up,
           rwkv_k_k, rwkv_k_a, rwkv_r_k, rwkv_gn_g, rwkv_gn_b,
           mla_q_norm, mla_w_uq, mla_kv_norm, mla_w_ukv, final_norm):
    n_batch, t_len, d = x.shape
    ctx_len = ctx.shape[1]
    depth = ada_w.shape[0]
    assert ctx_len == CHUNK and n_batch * ctx_len == ROW_TILE and t_len % ROW_TILE == 0
    assert c.shape[0] + 1 <= SUBLANE
    ctx_rows = n_batch * ctx_len
    cpb = t_len // CHUNK
    tpb = t_len // ROW_TILE
    g = GROUP_W

    cvec = jnp.concatenate([c_ctx[None, :], c, jnp.zeros((SUBLANE - 1 - n_batch, d), F32)], axis=0)
    mods_all = _ada_call(cvec, ada_w, ada_b).reshape(depth, SUBLANE, N_MOD, d)

    cos_tab, sin_tab = _rope_tables(t_len, ctx_len)
    n_keys = t_len + ctx_len
    tk = _kv_tile(n_keys)
    tq = ROW_TILE
    zeros = jnp.zeros

    a_cols, b_cols, c_cols = 2 * g, 2 * g, 3 * g + 256
    w_a = w_in[:, :, :a_cols].astype(BF16)
    w_b = w_in[:, :, a_cols:a_cols + b_cols].astype(BF16)
    w_c = w_in[:, :, a_cols + b_cols:a_cols + b_cols + c_cols].astype(BF16)
    w_dq = w_in[:, :, a_cols + b_cols + c_cols:]
    w_kr = w_dq[:, :, Q_LORA + KV_LORA:]
    z64 = zeros((depth, d, QK_NOPE), F32)
    z32 = zeros((depth, d, LANE - QK_NOPE - QK_ROPE), F32)
    w_d = jnp.concatenate([w_dq[:, :, :Q_LORA + KV_LORA], z64, w_kr, z32, z64, _rot_cols(w_kr), z32],
                          axis=2).astype(BF16)

    hw = MLA_HEADS * LANE
    wq = mla_w_uq.reshape(depth, Q_LORA, MLA_HEADS, QK_NOPE + QK_ROPE)
    zq = zeros((depth, Q_LORA, MLA_HEADS, LANE - QK_NOPE - QK_ROPE), F32)
    wqm = jnp.concatenate([wq, zq], axis=-1).reshape(depth, Q_LORA, hw).astype(BF16)
    wqr = jnp.concatenate([zeros((depth, Q_LORA, MLA_HEADS, QK_NOPE), F32), _rot_cols(wq[..., QK_NOPE:]), zq],
                          axis=-1).reshape(depth, Q_LORA, hw).astype(BF16)
    wkv = mla_w_ukv.reshape(depth, KV_LORA, MLA_HEADS, QK_NOPE + V_HEAD)
    zk = zeros((depth, KV_LORA, MLA_HEADS, LANE - QK_NOPE), F32)
    wk = jnp.concatenate([wkv[..., :QK_NOPE], zk], axis=-1).reshape(depth, KV_LORA, hw).astype(BF16)
    wv = jnp.concatenate([wkv[..., QK_NOPE:], zk], axis=-1).reshape(depth, KV_LORA, hw).astype(BF16)

    wo_d = w_out[:, 3 * g:].reshape(depth, MLA_HEADS, V_HEAD, d)
    wo_d = jnp.concatenate([wo_d, zeros((depth, MLA_HEADS, LANE - V_HEAD, d), F32)], axis=2)
    wo_parts = [w_out[:, 0:g].astype(BF16), w_out[:, g:2 * g].astype(BF16), w_out[:, 2 * g:3 * g].astype(BF16),
                wo_d.reshape(depth, hw, d).astype(BF16)]

    row = lambda v: v.reshape(v.shape[:-1] + (1, v.shape[-1]))
    conv_rows = (row(cv_dw_b), row(cv_ln_g), row(cv_ln_b))
    lru_gate_w = jnp.concatenate([_block_diag(lru_wa), _block_diag(lru_wx)], axis=-1).astype(BF16)
    lru_gate_b = jnp.concatenate([lru_ba, lru_bx], axis=-1)
    lru_p = (row(lru_conv_b), lru_gate_w, row(lru_gate_b), row(lru_lambda))
    mla_p = (row(mla_q_norm), row(mla_kv_norm), wqm, wqr, wk, wv)
    ffn_w = [(ffn1_w13.astype(BF16), ffn1_w2.astype(BF16)), (ffn2_w13.astype(BF16), ffn2_w2.astype(BF16))]
    rwkv_p = _rwkv_params(rwkv_mu_prev, rwkv_mu_next, rwkv_w0, rwkv_w_up, rwkv_a0, rwkv_a_up, rwkv_g_up,
                          rwkv_k_k, rwkv_k_a, rwkv_r_k, rwkv_gn_g, rwkv_gn_b)

    ctx_flat = ctx.reshape(ctx_rows, d)
    x_flat = x.reshape(n_batch * t_len, d)
    xs = None
    out = None
    for l in range(depth):
        mods = mods_all[l]
        last = l == depth - 1

        xs = _ffn_call((ctx_flat, x_flat) if l == 0 else xs, mods, *ffn_w[0], layer=l, row0=0,
                       tiles_per_batch=tpb, ctx_rows=ctx_rows)

        u_a, u_b, u_c, q, k, v = _win_call(xs, mods, w_a, w_b, w_c, w_d, cos_tab, sin_tab, mla_p,
                                           layer=l, n_batch=n_batch, cpb=cpb)

        y_a = _conv_call(u_a, cv_dw_w, *conv_rows, layer=l, n_batch=n_batch, cpb=cpb)

        scan = dict(layer=l, n_batch=n_batch, cpb=cpb)
        h_rev, y_rev, bonus_rev = _scan_mixers_call(
            [_lru_call(u_b, lru_conv_w, *lru_p, None, reverse=True, **scan),
             _rwkv_call(u_c, rwkv_p, None, None, reverse=True, **scan)], n_steps=cpb + 1, name="scan_rev")
        y_b, y_c = _scan_mixers_call(
            [_lru_call(u_b, lru_conv_w, *lru_p, h_rev, reverse=False, **scan),
             _rwkv_call(u_c, rwkv_p, y_rev, bonus_rev, reverse=False, **scan)], n_steps=cpb + 1, name="scan_fwd")
        y_b = y_b.reshape(-1, g)
        y_c = y_c.reshape(-1, g)

        q_off = ctx_rows // tq
        o_lat = _attn_call(q, k, v, out_rows=n_batch * t_len, n_batch=n_batch, tq=tq, tk=tk,
                           n_q=t_len // tq, n_kv=n_keys // tk,
                           q_block=lambda b, qi: q_off + b * (t_len // tq) + qi,
                           kv_block=lambda b, kj: b * (n_keys // tk) + kj,
                           o_block=lambda b, qi: b * (t_len // tq) + qi, name="mla_attn")
        o_ctx = None if last else _attn_call(
            q, k, v, out_rows=ctx_rows, n_batch=n_batch, tq=CHUNK, tk=CHUNK, n_q=1, n_kv=1,
            q_block=lambda b, qi: b, kv_block=lambda b, kj: b * (cpb + 1) + cpb,
            o_block=lambda b, qi: b, name="mla_attn_ctx")

        mix = (y_a, y_b, y_c, o_lat, o_ctx, wo_parts)
        if last:
            out = _ffn_call(xs, mods, *ffn_w[1], layer=l, row0=6, tiles_per_batch=tpb, ctx_rows=ctx_rows,
                            mix=mix, final_gain=final_norm)
        else:
            xs = _ffn_call(xs, mods, *ffn_w[1], layer=l, row0=6, tiles_per_batch=tpb, ctx_rows=ctx_rows, mix=mix)

    return out.reshape(n_batch, t_len, d)
```
